```python
import jax, jax.numpy as jnp
from jax import lax
import numpy as np

D_MODEL = 1024
BATCH = 4
SEQ = 8192
DEPTH = 2
DEC_BATCH = 32
DEC_SEQ = 8
PAST_LEN = 16384
PAGE_SIZE = 128

H_A = 8
DH_A = 64
W_A = H_A * DH_A
H_B = 8
DH_B = 64
W_B = H_B * DH_B
R_DECAY = 64
R_ICLR = 64
R_GATE = 128
C_FOX = 3 * W_A + H_A
C_RWKV = 3 * W_B + R_DECAY + R_ICLR + R_GATE
C_GATES = 2 * D_MODEL
C_IN = C_FOX + C_RWKV + C_GATES
D_FF = 11 * D_MODEL // 4
CONV_W = 3
Q_BLOCK = 128
ALPHA = (2.0 * DEPTH) ** 0.25
BETA = (8.0 * DEPTH) ** -0.25
LN_EPS = 1e-5
GN_EPS = 64e-5

IN_SPLIT = [W_A, 2 * W_A, 3 * W_A, C_FOX, C_FOX + C_RWKV, C_FOX + C_RWKV + D_MODEL]
RWKV_SPLIT = [W_B, 2 * W_B, 3 * W_B, 3 * W_B + R_DECAY, 3 * W_B + R_DECAY + R_ICLR]

kernel_name = "fox_rwkv7_gated_merge_convffn_deepnorm_step"


def layer_norm(x, w, b):
    xf = x.astype(jnp.float32)
    mu = jnp.mean(xf, axis=-1, keepdims=True)
    var = jnp.mean(jnp.square(xf - mu), axis=-1, keepdims=True)
    return ((xf - mu) * lax.rsqrt(var + LN_EPS)).astype(x.dtype) * w + b


def fox_attend_blocks(q, k, v, logf):
    bsz, t = q.shape[:2]
    nb = t // Q_BLOCK
    c = jnp.transpose(jnp.cumsum(logf, axis=1), (0, 2, 1))
    q_blocks = jnp.transpose(q.reshape(bsz, nb, Q_BLOCK, H_A, DH_A), (1, 0, 2, 3, 4))
    c_blocks = jnp.transpose(c.reshape(bsz, H_A, nb, Q_BLOCK), (2, 0, 1, 3))
    kpos = jnp.arange(t)
    scale = DH_A ** -0.5

    def block(args):
        i, qi, ci = args
        s = jnp.einsum('bqhd,bkhd->bhqk', qi, k, preferred_element_type=jnp.float32) * scale
        s = s + (ci[..., :, None] - c[:, :, None, :])
        qpos = i * Q_BLOCK + jnp.arange(Q_BLOCK)
        s = jnp.where(kpos[None, :] <= qpos[:, None], s, -jnp.inf)
        p = jax.nn.softmax(s, axis=-1)
        return jnp.einsum('bhqk,bkhd->bqhd', p.astype(v.dtype), v)

    o = lax.map(block, (jnp.arange(nb), q_blocks, c_blocks))
    return jnp.transpose(o, (1, 0, 2, 3, 4)).reshape(bsz, t, H_A, DH_A)


def fox_attend_cached(q, k, v, logf, k_past, v_past, logf_past):
    past = k_past.shape[1]
    t = q.shape[1]
    k_all = jnp.concatenate([k_past.astype(k.dtype), k], axis=1)
    v_all = jnp.concatenate([v_past.astype(v.dtype), v], axis=1)
    c = jnp.cumsum(jnp.concatenate([logf_past.astype(jnp.float32), logf], axis=1), axis=1)
    c = jnp.transpose(c, (0, 2, 1))
    s = jnp.einsum('bqhd,bkhd->bhqk', q, k_all, preferred_element_type=jnp.float32) * (DH_A ** -0.5)
    s = s + (c[:, :, past:, None] - c[:, :, None, :])
    qpos = past + jnp.arange(t)
    kpos = jnp.arange(past + t)
    s = jnp.where(kpos[None, :] <= qpos[:, None], s, -jnp.inf)
    p = jax.nn.softmax(s, axis=-1)
    return jnp.einsum('bhqk,bkhd->bqhd', p.astype(v.dtype), v_all)


def wkv7_step(state, inp):
    r_t, w_t, k_t, v_t, a_t, b_t = inp
    sa = jnp.einsum('bhij,bhj->bhi', state, a_t)
    state = (state * w_t[:, :, None, :] + sa[..., None] * b_t[:, :, None, :]
             + v_t[..., None] * k_t[:, :, None, :])
    y = jnp.einsum('bhij,bhj->bhi', state, r_t)
    return state, y


def rwkv7_mix(p, prev, s0, mu, w0, w_up, a0, a_up, g_up, kk_scale, ka_scale, r_k, gn_w, gn_b):
    bsz, t = p.shape[:2]
    shifted = jnp.concatenate([prev[:, None, :].astype(p.dtype), p[:, :-1]], axis=1)
    xs = p + mu * (shifted - p)
    r, k, v, wd, ad, gd = jnp.split(xs, RWKV_SPLIT, axis=-1)
    w_raw = -jax.nn.softplus(-(w0 + jnp.tanh(wd) @ w_up)) - 0.5
    decay = jnp.exp(-jnp.exp(w_raw.astype(jnp.float32)))
    a = jax.nn.sigmoid(a0 + ad @ a_up)
    g = jax.nn.sigmoid(gd) @ g_up

    def heads(z):
        return z.reshape(bsz, t, H_B, DH_B).astype(jnp.float32)

    r, k, v, decay, a = heads(r), heads(k), heads(v), heads(decay), heads(a)
    kk = k * kk_scale.reshape(H_B, DH_B).astype(jnp.float32)
    kk = kk / jnp.maximum(jnp.sqrt(jnp.sum(jnp.square(kk), axis=-1, keepdims=True)), 1e-12)
    k = k * (1.0 + (a - 1.0) * ka_scale.reshape(H_B, DH_B).astype(jnp.float32))
    seq_major = lambda z: jnp.moveaxis(z, 1, 0)
    s_fin, y = lax.scan(wkv7_step, s0.astype(jnp.float32),
                        (seq_major(r), seq_major(decay), seq_major(k), seq_major(v),
                         seq_major(-kk), seq_major(kk * a)))
    y = jnp.moveaxis(y, 0, 1)
    mu_y = jnp.mean(y, axis=-1, keepdims=True)
    var_y = jnp.mean(jnp.square(y - mu_y), axis=-1, keepdims=True)
    yn = (y - mu_y) * lax.rsqrt(var_y + GN_EPS)
    yn = yn * gn_w.reshape(H_B, DH_B).astype(jnp.float32) + gn_b.reshape(H_B, DH_B).astype(jnp.float32)
    yn = yn + jnp.sum(r * k * r_k.astype(jnp.float32), axis=-1, keepdims=True) * v
    out = (yn.reshape(bsz, t, W_B) * g.astype(jnp.float32)).astype(p.dtype)
    return out, s_fin, p[:, -1]


def conv_ffn(x, buf, w_up, conv_w, conv_b, w_down):
    t = x.shape[1]
    u, gate = jnp.split(x @ w_up, [D_FF], axis=-1)
    upad = jnp.concatenate([buf.astype(u.dtype), u], axis=1)
    conv = conv_b + sum(conv_w[j] * upad[:, j:j + t] for j in range(CONV_W))
    h = jax.nn.gelu(conv) * gate
    return h @ w_down, upad[:, -(CONV_W - 1):]


def setup_inputs(seed: int = 0) -> dict:
    key = jax.random.key(seed)
    ks = iter(list(jax.random.split(key, 48)))
    nrm = lambda shape, s=1.0: s * jax.random.normal(next(ks), shape, jnp.float32)
    n_pages = PAST_LEN // PAGE_SIZE
    used = DEC_BATCH * n_pages
    n_pool = used + max(1, used // 4)

    x_prompt = nrm((BATCH, SEQ, D_MODEL))
    x_sample = nrm((DEC_BATCH, DEC_SEQ, D_MODEL))
    cache_k = nrm((DEPTH, n_pool, PAGE_SIZE, H_A, DH_A))
    cache_v = nrm((DEPTH, n_pool, PAGE_SIZE, H_A, DH_A), BETA)
    cache_logf = jax.nn.log_sigmoid(3.0 + nrm((DEPTH, n_pool, PAGE_SIZE, H_A), 0.3))
    state_wkv = nrm((DEPTH, DEC_BATCH, H_B, DH_B, DH_B), 0.1)
    state_shift = nrm((DEPTH, DEC_BATCH, C_RWKV))
    state_conv = nrm((DEPTH, DEC_BATCH, CONV_W - 1, D_FF))
    page_table = jax.random.permutation(next(ks), n_pool)[:used].reshape(DEC_BATCH, n_pages).astype(jnp.int32)

    col_scale = jnp.concatenate([
        jnp.ones((2 * W_A,), jnp.float32), BETA * jnp.ones((W_A,), jnp.float32),
        0.1 * jnp.ones((H_A,), jnp.float32),
        jnp.ones((2 * W_B,), jnp.float32), BETA * jnp.ones((W_B,), jnp.float32),
        jnp.ones((R_DECAY + R_ICLR + R_GATE + C_GATES,), jnp.float32)])
    w_in = nrm((DEPTH, D_MODEL, C_IN), D_MODEL ** -0.5) * col_scale
    b_forget = 3.0 + nrm((DEPTH, H_A), 0.1)
    mu_shift = jax.random.uniform(next(ks), (DEPTH, C_RWKV), jnp.float32, 0.1, 0.9)
    decay_w0 = -1.0 + nrm((DEPTH, W_B), 0.5)
    decay_up = nrm((DEPTH, R_DECAY, W_B), 0.5 * R_DECAY ** -0.5)
    iclr_a0 = nrm((DEPTH, W_B), 0.5)
    iclr_up = nrm((DEPTH, R_ICLR, W_B), R_ICLR ** -0.5)
    gate_up = nrm((DEPTH, R_GATE, W_B), 2.0 * R_GATE ** -0.5)
    k_k = 0.85 + nrm((DEPTH, W_B), 0.05)
    k_a = 1.0 + nrm((DEPTH, W_B), 0.05)
    r_k = nrm((DEPTH, H_B, DH_B), 0.1)
    gn_w = 1.0 + nrm((DEPTH, W_B), 0.05)
    gn_b = nrm((DEPTH, W_B), 0.02)
    w_branch_a = nrm((DEPTH, W_A, D_MODEL), W_A ** -0.5)
    w_branch_b = nrm((DEPTH, W_B, D_MODEL), W_B ** -0.5)
    w_out = nrm((DEPTH, D_MODEL, D_MODEL), BETA * D_MODEL ** -0.5)
    ln1_w = 1.0 + nrm((DEPTH, D_MODEL), 0.05)
    ln1_b = nrm((DEPTH, D_MODEL), 0.02)
    w_up = nrm((DEPTH, D_MODEL, 2 * D_FF), D_MODEL ** -0.5)
    conv_w = nrm((DEPTH, CONV_W, D_FF), CONV_W ** -0.5)
    conv_b = nrm((DEPTH, D_FF), 0.02)
    w_down = nrm((DEPTH, D_FF, D_MODEL), BETA * D_FF ** -0.5)
    ln2_w = 1.0 + nrm((DEPTH, D_MODEL), 0.05)
    ln2_b = nrm((DEPTH, D_MODEL), 0.02)
    return {"x_prompt": x_prompt, "x_sample": x_sample, "cache_k": cache_k, "cache_v": cache_v,
            "cache_logf": cache_logf, "state_wkv": state_wkv, "state_shift": state_shift,
            "state_conv": state_conv, "page_table": page_table, "w_in": w_in, "b_forget": b_forget,
            "mu_shift": mu_shift, "decay_w0": decay_w0, "decay_up": decay_up, "iclr_a0": iclr_a0,
            "iclr_up": iclr_up, "gate_up": gate_up, "k_k": k_k, "k_a": k_a, "r_k": r_k,
            "gn_w": gn_w, "gn_b": gn_b, "w_branch_a": w_branch_a, "w_branch_b": w_branch_b,
            "w_out": w_out, "ln1_w": ln1_w, "ln1_b": ln1_b, "w_up": w_up, "conv_w": conv_w,
            "conv_b": conv_b, "w_down": w_down, "ln2_w": ln2_w, "ln2_b": ln2_b}


def reference(x_prompt, x_sample, cache_k, cache_v, cache_logf, state_wkv, state_shift, state_conv,
              page_table, w_in, b_forget, mu_shift, decay_w0, decay_up, iclr_a0, iclr_up, gate_up,
              k_k, k_a, r_k, gn_w, gn_b, w_branch_a, w_branch_b, w_out, ln1_w, ln1_b,
              w_up, conv_w, conv_b, w_down, ln2_w, ln2_b):
    n_pages = page_table.shape[1]

    def run_group(x, attend, shift0, wkv0, conv0):
        bsz, t = x.shape[:2]
        ks, vs, lfs, wkvs, shifts, convs = [], [], [], [], [], []
        for l in range(DEPTH):
            p = x @ w_in[l]
            q, k, v, pf, p_rwkv, pga, pgb = jnp.split(p, IN_SPLIT, axis=-1)
            q = q.reshape(bsz, t, H_A, DH_A)
            k = k.reshape(bsz, t, H_A, DH_A)
            v = v.reshape(bsz, t, H_A, DH_A)
            logf = jax.nn.log_sigmoid(pf.astype(jnp.float32) + b_forget[l].astype(jnp.float32))
            o_a = attend(l, q, k, v, logf).reshape(bsz, t, W_A)
            o_b, wkv_l, shift_l = rwkv7_mix(p_rwkv, shift0[l], wkv0[l], mu_shift[l], decay_w0[l],
                                            decay_up[l], iclr_a0[l], iclr_up[l], gate_up[l],
                                            k_k[l], k_a[l], r_k[l], gn_w[l], gn_b[l])
            merged = (jax.nn.sigmoid(pga) * (o_a @ w_branch_a[l])
                      + jax.nn.sigmoid(pgb) * (o_b @ w_branch_b[l]))
            x = layer_norm(ALPHA * x + merged @ w_out[l], ln1_w[l], ln1_b[l])
            f, conv_l = conv_ffn(x, conv0[l], w_up[l], conv_w[l], conv_b[l], w_down[l])
            x = layer_norm(ALPHA * x + f, ln2_w[l], ln2_b[l])
            ks.append(k); vs.append(v); lfs.append(logf)
            wkvs.append(wkv_l); shifts.append(shift_l); convs.append(conv_l)
        return (x, jnp.stack(ks), jnp.stack(vs), jnp.stack(lfs), jnp.stack(wkvs),
                jnp.stack(shifts), jnp.stack(convs))

    def attend_prompt(l, q, k, v, logf):
        return fox_attend_blocks(q, k, v, logf)

    def attend_sample(l, q, k, v, logf):
        nb = page_table.shape[0]
        kp = cache_k[l][page_table].reshape(nb, n_pages * PAGE_SIZE, H_A, DH_A)
        vp = cache_v[l][page_table].reshape(nb, n_pages * PAGE_SIZE, H_A, DH_A)
        lp = cache_logf[l][page_table].reshape(nb, n_pages * PAGE_SIZE, H_A)
        return fox_attend_cached(q, k, v, logf, kp, vp, lp)

    bp = x_prompt.shape[0]
    shift_zero = jnp.zeros((DEPTH, bp, C_RWKV), x_prompt.dtype)
    wkv_zero = jnp.zeros((DEPTH, bp, H_B, DH_B, DH_B), jnp.float32)
    conv_zero = jnp.zeros((DEPTH, bp, CONV_W - 1, D_FF), x_prompt.dtype)
    (y_prompt, k_prompt, v_prompt, logf_prompt, wkv_prompt, shift_prompt,
     conv_prompt) = run_group(x_prompt, attend_prompt, shift_zero, wkv_zero, conv_zero)
    (y_sample, k_sample, v_sample, logf_sample, wkv_sample, shift_sample,
     conv_sample) = run_group(x_sample, attend_sample, state_shift, state_wkv, state_conv)
    return (y_prompt, y_sample, k_prompt, v_prompt, logf_prompt, wkv_prompt, shift_prompt, conv_prompt,
            k_sample, v_sample, logf_sample, wkv_sample, shift_sample, conv_sample)
```

```python
import functools
import math

import jax
import jax.numpy as jnp
from jax import lax
from jax.experimental import pallas as pl
from jax.experimental.pallas import tpu as pltpu

F32 = jnp.float32
BF16 = jnp.bfloat16

N_HEADS = 8
HEAD_DIM = 64
WIDTH = N_HEADS * HEAD_DIM
R_DECAY = 64
R_ICLR = 64
R_GATE = 128
C_RWKV = 3 * WIDTH + R_DECAY + R_ICLR + R_GATE
CONV_W = 3
PAGE_SIZE = 128
LN_EPS = 1e-5
GN_EPS = 64e-5
NEG_BIG = -1e30

LANES = 128
ROW_TILE = 256
RWKV_CHUNK = 64
PAGES_PER_STEP = 8
VMEM_LIMIT_BYTES = 56 * 1024 * 1024

_OFF_Q, _OFF_K, _OFF_V, _OFF_F = 0, WIDTH, 2 * WIDTH, 3 * WIDTH
_OFF_RWKV = 3 * WIDTH + LANES
_OFF_GA = _OFF_RWKV + C_RWKV


def _iota(shape, dim):
    return lax.broadcasted_iota(jnp.int32, shape, dim)


def _dot(a, b):
    return jnp.dot(a.astype(BF16), b.astype(BF16), preferred_element_type=F32)


def _dot_nt(a, b):
    return lax.dot_general(a.astype(BF16), b.astype(BF16), (((1,), (1,)), ((), ())),
                           preferred_element_type=F32)


def _dot_tn(a, b):
    return lax.dot_general(a.astype(BF16), b.astype(BF16), (((0,), (0,)), ((), ())),
                           preferred_element_type=F32)


def _split3(x):
    hi = x.astype(BF16)
    r1 = x - hi.astype(F32)
    mid = r1.astype(BF16)
    lo = (r1 - mid.astype(F32)).astype(BF16)
    return hi, mid, lo


def _dot_sel_rhs(x, sel):
    hi, mid, lo = _split3(x)
    d = lambda p: jnp.dot(p, sel, preferred_element_type=F32)
    return d(hi) + (d(mid) + d(lo))


def _dot_sel_lhs(sel, x):
    hi, mid, lo = _split3(x)
    d = lambda p: jnp.dot(sel, p, preferred_element_type=F32)
    return d(hi) + (d(mid) + d(lo))


def _sigmoid(x):
    return 1.0 / (1.0 + jnp.exp(-x))


def _log_sigmoid(x):
    return jnp.minimum(x, 0.0) - jnp.log1p(jnp.exp(-jnp.abs(x)))


def _layer_norm(x, w, b):
    mu = jnp.mean(x, axis=-1, keepdims=True)
    d = x - mu
    var = jnp.mean(d * d, axis=-1, keepdims=True)
    return d * lax.rsqrt(var + LN_EPS) * w + b


def _params(*sem):
    return pltpu.CompilerParams(dimension_semantics=sem, vmem_limit_bytes=VMEM_LIMIT_BYTES)


def _const_spec(shape):
    zeros = (0,) * len(shape)
    return pl.BlockSpec(shape, lambda *_: zeros, pipeline_mode=pl.Buffered(1))


def _in_proj_kernel(x_ref, w_ref, bf_ref, *refs, tm, emit_bf):
    if emit_bf:
        (q_ref, k_ref, v_ref, kbf_ref, vt_ref, logf_ref, c_ref, prw_ref, ga_ref, gb_ref, carry_ref) = refs
    else:
        (q_ref, k_ref, v_ref, logf_ref, c_ref, prw_ref, ga_ref, gb_ref, carry_ref) = refs
    t = pl.program_id(1)
    x = x_ref[0].astype(BF16)

    def proj(lo, hi):
        return jnp.dot(x, w_ref[:, lo:hi], preferred_element_type=F32)

    q = proj(_OFF_Q, _OFF_Q + WIDTH)
    q_ref[0] = (q * HEAD_DIM ** -0.5).astype(BF16)
    k = proj(_OFF_K, _OFF_K + WIDTH)
    k_ref[0] = k
    v = proj(_OFF_V, _OFF_V + WIDTH)
    v_ref[0] = v
    if emit_bf:
        kbf_ref[0] = k.astype(BF16)
        vt_ref[0, 0] = v.T.astype(BF16)

    z = proj(_OFF_F, _OFF_F + LANES) + bf_ref[...]
    lane = _iota((tm, LANES), 1)
    lf = jnp.where(lane < N_HEADS, _log_sigmoid(z), 0.0)
    logf_ref[0] = lf[:, :N_HEADS]

    @pl.when(t == 0)
    def _():
        carry_ref[...] = jnp.zeros_like(carry_ref)

    tril = jnp.where(_iota((tm, tm), 1) <= _iota((tm, tm), 0), 1.0, 0.0).astype(BF16)
    cs = _dot_sel_lhs(tril, lf) + carry_ref[...]
    carry_ref[...] = cs[tm - 1:tm, :]
    c_ref[0] = cs[:, :N_HEADS]

    prw_ref[0] = proj(_OFF_RWKV, _OFF_RWKV + C_RWKV)
    d_model = ga_ref.shape[-1]
    ga_ref[0] = _sigmoid(proj(_OFF_GA, _OFF_GA + d_model))
    gb_ref[0] = _sigmoid(proj(_OFF_GA + d_model, _OFF_GA + 2 * d_model))


def _in_proj(x, w, bf, *, emit_bf):
    bsz, t, d = x.shape
    tm = min(t, ROW_TILE)
    nt = t // tm
    c_all = w.shape[1]
    row = lambda width: pl.BlockSpec((1, tm, width), lambda b, i: (b, i, 0))
    out_shape = [jax.ShapeDtypeStruct((bsz, t, WIDTH), BF16),
                 jax.ShapeDtypeStruct((bsz, t, WIDTH), F32),
                 jax.ShapeDtypeStruct((bsz, t, WIDTH), F32)]
    out_specs = [row(WIDTH), row(WIDTH), row(WIDTH)]
    if emit_bf:
        out_shape += [jax.ShapeDtypeStruct((bsz, t, WIDTH), BF16),
                      jax.ShapeDtypeStruct((bsz, nt, WIDTH, tm), BF16)]
        out_specs += [row(WIDTH), pl.BlockSpec((1, 1, WIDTH, tm), lambda b, i: (b, i, 0, 0))]
    out_shape += [jax.ShapeDtypeStruct((bsz, t, N_HEADS), F32),
                  jax.ShapeDtypeStruct((bsz, t, N_HEADS), F32),
                  jax.ShapeDtypeStruct((bsz, t, C_RWKV), F32),
                  jax.ShapeDtypeStruct((bsz, t, d), F32),
                  jax.ShapeDtypeStruct((bsz, t, d), F32)]
    out_specs += [row(N_HEADS), row(N_HEADS), row(C_RWKV), row(d), row(d)]
    return pl.pallas_call(
        functools.partial(_in_proj_kernel, tm=tm, emit_bf=emit_bf),
        grid=(bsz, nt),
        in_specs=[row(d), _const_spec((d, c_all)), _const_spec((1, LANES))],
        out_specs=out_specs,
        out_shape=out_shape,
        scratch_shapes=[pltpu.VMEM((1, LANES), F32)],
        compiler_params=_params("arbitrary", "arbitrary"),
        name="in_proj",
    )(x, w, bf)


def _fox_prompt_kernel(q_ref, k_ref, vt_ref, crow_ref, ccol_ref, o_ref, *, tq):
    i = pl.program_id(2)
    q = q_ref[0]
    lane = _iota((tq, LANES), 1)
    causal = _iota((tq, tq), 0) <= _iota((tq, tq), 1)
    outs = []
    for h in range(2):
        in_head = (lane >= HEAD_DIM * h) & (lane < HEAD_DIM * (h + 1))
        qh = jnp.where(in_head, q, jnp.zeros_like(q))
        ci = crow_ref[0, 0, h:h + 1, :]

        def tile(j, carry, masked, qh=qh, ci=ci, h=h):
            m, l, acc = carry
            start = pl.multiple_of(j * tq, tq)
            kt = k_ref[0, pl.ds(start, tq), :]
            s = _dot_nt(kt, qh)
            cj = ccol_ref[0, 0, pl.ds(start, tq), h:h + 1]
            s = s + (ci - cj)
            if masked:
                s = jnp.where(causal, s, NEG_BIG)
            m_new = jnp.maximum(m, jnp.max(s, axis=0, keepdims=True))
            alpha = jnp.exp(m - m_new)
            p = jnp.exp(s - m_new)
            l = alpha * l + jnp.sum(p, axis=0, keepdims=True)
            vt = vt_ref[0, j, HEAD_DIM * h:HEAD_DIM * (h + 1), :]
            acc = alpha * acc + jnp.dot(vt, p.astype(BF16), preferred_element_type=F32)
            return m_new, l, acc

        carry = (jnp.full((1, tq), NEG_BIG, F32), jnp.zeros((1, tq), F32),
                 jnp.zeros((HEAD_DIM, tq), F32))
        carry = lax.fori_loop(0, i, lambda j, c: tile(j, c, False), carry)
        _, l, acc = tile(i, carry, True)
        outs.append(acc / l)
    o_ref[0] = jnp.concatenate(outs, axis=0).T


def _fox_prompt(q_bf, k_bf, vt_bf, c):
    bsz, t, _ = q_bf.shape
    tq = vt_bf.shape[-1]
    nq = t // tq
    n_pairs = N_HEADS // 2
    c_row = c.transpose(0, 2, 1).reshape(bsz, n_pairs, 2, t)
    c_col = c.reshape(bsz, t, n_pairs, 2).transpose(0, 2, 1, 3)
    return pl.pallas_call(
        functools.partial(_fox_prompt_kernel, tq=tq),
        grid=(bsz, n_pairs, nq),
        in_specs=[pl.BlockSpec((1, tq, LANES), lambda b, hp, i: (b, i, hp)),
                  pl.BlockSpec((1, t, LANES), lambda b, hp, i: (b, 0, hp)),
                  pl.BlockSpec((1, nq, LANES, tq), lambda b, hp, i: (b, 0, hp, 0)),
                  pl.BlockSpec((1, 1, 2, tq), lambda b, hp, i: (b, hp, 0, i)),
                  pl.BlockSpec((1, 1, t, 2), lambda b, hp, i: (b, hp, 0, 0))],
        out_specs=pl.BlockSpec((1, tq, LANES), lambda b, hp, i: (b, i, hp)),
        out_shape=jax.ShapeDtypeStruct((bsz, t, WIDTH), F32),
        compiler_params=_params("arbitrary", "arbitrary", "arbitrary"),
        name="fox_prompt",
    )(q_bf, k_bf, vt_bf, c_row, c_col)


def _fox_cached_kernel(pt_ref, q_ref, knew_ref, vnew_ref, cq_ref, ck_ref, *refs, n_group):
    del pt_ref
    kp = refs[:n_group]
    vp = refs[n_group:2 * n_group]
    lp = refs[2 * n_group:3 * n_group]
    o_ref, m_ref, l_ref, acc_ref, carry_ref = refs[3 * n_group:]
    step = pl.program_id(1)
    t_new = q_ref.shape[1]
    rows = N_HEADS * t_new
    head_of_row = _iota((rows, WIDTH), 0) // t_new
    own_head = head_of_row == _iota((rows, WIDTH), 1) // HEAD_DIM
    q = q_ref[0]
    qx = jnp.where(own_head, jnp.concatenate([q] * N_HEADS, axis=0), jnp.zeros((), q.dtype))

    @pl.when(step == 0)
    def _():
        s = _dot_nt(qx, knew_ref[0])
        s = s + (cq_ref[0] - ck_ref[0])
        qpos = _iota((rows, t_new), 0) % t_new
        s = jnp.where(_iota((rows, t_new), 1) <= qpos, s, NEG_BIG)
        m = jnp.max(s, axis=1, keepdims=True)
        p = jnp.exp(s - m)
        m_ref[...] = m
        l_ref[...] = jnp.sum(p, axis=1, keepdims=True)
        acc_ref[...] = _dot(p, vnew_ref[0])
        carry_ref[...] = jnp.broadcast_to(cq_ref[0], carry_ref.shape)

    key_r = _iota((PAGE_SIZE, 2 * PAGE_SIZE), 0)
    key_c = _iota((PAGE_SIZE, 2 * PAGE_SIZE), 1)
    later_and_all = jnp.where((key_c >= PAGE_SIZE) | (key_r > key_c), 1.0, 0.0).astype(BF16)
    expand = jnp.where(_iota((N_HEADS, LANES), 0) == _iota((N_HEADS, LANES), 1) // t_new,
                       1.0, 0.0).astype(BF16)
    carry = carry_ref[...]
    scores = []
    for g in range(n_group):
        s = _dot_nt(qx, kp[g][0])
        lf_rows = _dot_sel_rhs(lp[g][0], expand).T[:rows]
        both = _dot_sel_rhs(lf_rows, later_and_all)
        scores.append(s + (both[:, :PAGE_SIZE] + carry))
        carry = carry + both[:, PAGE_SIZE:]
    carry_ref[...] = carry

    smax = scores[0]
    for s in scores[1:]:
        smax = jnp.maximum(smax, s)
    m_old = m_ref[...]
    m_new = jnp.maximum(m_old, jnp.max(smax, axis=1, keepdims=True))
    alpha = jnp.exp(m_old - m_new)
    l = alpha * l_ref[...]
    acc = alpha * acc_ref[...]
    for g in range(n_group):
        p = jnp.exp(scores[g] - m_new)
        l = l + jnp.sum(p, axis=1, keepdims=True)
        acc = acc + _dot(p, vp[g][0])
    m_ref[...] = m_new
    l_ref[...] = l
    acc_ref[...] = acc

    @pl.when(step == pl.num_programs(1) - 1)
    def _():
        o = jnp.where(own_head, acc / l, 0.0)
        out = o[0:t_new]
        for h in range(1, N_HEADS):
            out = out + o[h * t_new:(h + 1) * t_new]
        o_ref[0] = out


def _fox_cached(q_bf, k_new, v_new, c_new, cache_k, cache_v, cache_logf, pages):
    bsz, t_new, _ = q_bf.shape
    n_pages = pages.shape[1]
    n_group = math.gcd(PAGES_PER_STEP, n_pages)
    n_steps = n_pages // n_group
    rows = N_HEADS * t_new
    c_q = c_new.transpose(0, 2, 1).reshape(bsz, rows, 1)
    c_k = jnp.repeat(c_new.transpose(0, 2, 1), t_new, axis=1)

    def page_map(g):
        return lambda b, s, pt: (pt[b, n_pages - 1 - (s * n_group + g)], 0, 0)

    per_b = lambda shape: pl.BlockSpec((1,) + shape, lambda b, s, pt: (b, 0, 0))
    in_specs = [per_b((t_new, WIDTH)), per_b((t_new, WIDTH)), per_b((t_new, WIDTH)),
                per_b((rows, 1)), per_b((rows, t_new))]
    in_specs += [pl.BlockSpec((1, PAGE_SIZE, WIDTH), page_map(g)) for g in range(n_group)]
    in_specs += [pl.BlockSpec((1, PAGE_SIZE, WIDTH), page_map(g)) for g in range(n_group)]
    in_specs += [pl.BlockSpec((1, PAGE_SIZE, N_HEADS), page_map(g)) for g in range(n_group)]
    return pl.pallas_call(
        functools.partial(_fox_cached_kernel, n_group=n_group),
        grid_spec=pltpu.PrefetchScalarGridSpec(
            num_scalar_prefetch=1,
            grid=(bsz, n_steps),
            in_specs=in_specs,
            out_specs=pl.BlockSpec((1, t_new, WIDTH), lambda b, s, pt: (b, 0, 0)),
            scratch_shapes=[pltpu.VMEM((rows, 1), F32), pltpu.VMEM((rows, 1), F32),
                            pltpu.VMEM((rows, WIDTH), F32), pltpu.VMEM((rows, PAGE_SIZE), F32)]),
        out_shape=jax.ShapeDtypeStruct((bsz, t_new, WIDTH), F32),
        compiler_params=_params("arbitrary", "arbitrary"),
        name="fox_cached",
    )(pages, q_bf, k_new, v_new, c_q, c_k,
      *([cache_k] * n_group), *([cache_v] * n_group), *([cache_logf] * n_group))


def _unit_lower_inverse(n_mat, eye, blk_of_row, blk_of_col, chunk):
    base = 8
    same = lambda size: (blk_of_row // (size // base)) == (blk_of_col // (size // base))
    nb = jnp.where(same(base), n_mat, 0.0)
    x = eye + nb
    m = nb
    for _ in range(2):
        m = _dot(m, m)
        x = x + _dot(x, m)
    size = base
    while size < chunk:
        off = jnp.where(same(2 * size) & jnp.logical_not(same(size)), n_mat, 0.0)
        x = x + _dot(_dot(x, off), x)
        size *= 2
    return x


def _rwkv_kernel(p_ref, shift0_ref, wkv0_ref, mu_ref, w0_ref, lora_ref, a0_ref, gup_ref,
                 kk_ref, ka_ref, rk_ref, gnw_ref, gnb_ref,
                 o_ref, wkv_ref, shift_ref,
                 state_ref, prev_ref, pre_ref, y_ref, *, chunk, heads_per_group, tb):
    t = pl.program_id(1)
    gw = heads_per_group * HEAD_DIM
    n_groups = N_HEADS // heads_per_group
    gl = heads_per_group * chunk
    n_chunks = tb // chunk

    row_head = _iota((gw, gw), 0) // HEAD_DIM
    col_head = _iota((gw, gw), 1) // HEAD_DIM
    state_mask = row_head == col_head

    @pl.when(t == 0)
    def _():
        prev_ref[...] = shift0_ref[0]
        tile_lanes = jnp.where(_iota((HEAD_DIM, gw), 0) == _iota((HEAD_DIM, gw), 1) % HEAD_DIM,
                               1.0, 0.0).astype(BF16)
        for g in range(n_groups):
            s0 = wkv0_ref[0, g * gw:(g + 1) * gw, :]
            state_ref[g] = jnp.where(state_mask, _dot_sel_rhs(s0, tile_lanes), 0.0)

    p = p_ref[0]
    first_row = _iota((tb, C_RWKV), 0) == 0
    shifted = jnp.where(first_row, prev_ref[...], pltpu.roll(p, 1, 0))
    prev_ref[...] = p[tb - 1:tb, :]
    shift_ref[0] = p[tb - 1:tb, :]
    xs = p + mu_ref[...] * (shifted - p)
    r = xs[:, 0:WIDTH]
    k = xs[:, WIDTH:2 * WIDTH]
    v = xs[:, 2 * WIDTH:3 * WIDTH]
    wa = xs[:, 3 * WIDTH:3 * WIDTH + LANES]
    gd = xs[:, 3 * WIDTH + LANES:]
    lane = _iota((tb, LANES), 1)
    lora = _dot(jnp.where(lane < R_DECAY, jnp.tanh(wa), wa), lora_ref[...])
    w_raw = _log_sigmoid(w0_ref[...] + lora[:, :WIDTH]) - 0.5
    lw = -jnp.exp(w_raw)
    iclr = _sigmoid(a0_ref[...] + lora[:, WIDTH:])
    gate = _dot(_sigmoid(gd), gup_ref[...])

    seg_r = _iota((WIDTH, WIDTH), 0) // HEAD_DIM
    seg_c = _iota((WIDTH, WIDTH), 1) // HEAD_DIM
    head_ones = jnp.where(seg_r == seg_c, 1.0, 0.0).astype(BF16)

    kk = k * kk_ref[...]
    kk_norm = jnp.sqrt(_dot_sel_rhs(kk * kk, head_ones))
    kk = kk / jnp.maximum(kk_norm, 1e-12)
    k2 = k * (1.0 + (iclr - 1.0) * ka_ref[...])
    a = -kk
    b = kk * iclr

    blk_r = _iota((tb, tb), 0) // chunk
    blk_c = _iota((tb, tb), 1) // chunk
    same_chunk = blk_r == blk_c
    incl = jnp.where(same_chunk & (_iota((tb, tb), 1) <= _iota((tb, tb), 0)), 1.0, 0.0).astype(BF16)
    whole = jnp.where(same_chunk, 1.0, 0.0).astype(BF16)
    cum = _dot_sel_lhs(incl, lw)
    cum_end = _dot_sel_lhs(whole, lw)
    pre_ref[0] = a * jnp.exp(cum - lw)
    pre_ref[1] = r * jnp.exp(cum)
    pre_ref[2] = b * jnp.exp(-cum)
    pre_ref[3] = k2 * jnp.exp(-cum)
    pre_ref[4] = b * jnp.exp(cum_end - cum)
    pre_ref[5] = k2 * jnp.exp(cum_end - cum)
    pre_ref[6] = v
    pre_ref[7] = jnp.exp(cum_end)

    e_row_head = _iota((gl, gw), 0) // chunk
    e_col_head = _iota((gl, gw), 1) // HEAD_DIM
    expand_mask = e_row_head == e_col_head
    sq_r = _iota((gl, gl), 0)
    sq_c = _iota((gl, gl), 1)
    same_head = (sq_r // chunk) == (sq_c // chunk)
    strict = same_head & (sq_c < sq_r)
    lower = same_head & (sq_c <= sq_r)
    eye = jnp.where(sq_r == sq_c, 1.0, 0.0)
    blk8_r = sq_r // 8
    blk8_c = sq_c // 8

    def expand(x):
        return jnp.where(expand_mask, jnp.concatenate([x] * heads_per_group, axis=0), 0.0)

    def collapse(x):
        out = x[0:chunk]
        for h in range(1, heads_per_group):
            out = out + x[h * chunk:(h + 1) * chunk]
        return out

    def tile_rows(x):
        return jnp.concatenate([x] * heads_per_group, axis=0)

    def chunk_step(c, _):
        r0 = pl.multiple_of(c * chunk, chunk)
        for g in range(n_groups):
            cols = slice(g * gw, (g + 1) * gw)
            ld = lambda idx: pre_ref[idx, pl.ds(r0, chunk), cols]
            a_t, r_t, b_t, k_t, b_e, k_e, v_c = (ld(i) for i in range(7))
            p_end = pre_ref[7, pl.ds(r0, 1), cols]
            ax, rx, vx = expand(a_t), expand(r_t), expand(v_c)
            bt, kt = tile_rows(b_t), tile_rows(k_t)
            n_mat = jnp.where(strict, _dot_nt(ax, bt), 0.0)
            a_ak = jnp.where(strict, _dot_nt(ax, kt), 0.0)
            a_rb = jnp.where(lower, _dot_nt(rx, bt), 0.0)
            a_rk = jnp.where(lower, _dot_nt(rx, kt), 0.0)
            t_inv = _unit_lower_inverse(n_mat, eye, blk8_r, blk8_c, chunk)
            s0 = state_ref[g]
            rhs = expand(_dot_nt(a_t, s0)) + _dot(a_ak, vx)
            ux = _dot(t_inv, rhs)
            y = _dot_nt(r_t, s0) + collapse(_dot(a_rb, ux) + _dot(a_rk, vx))
            y_ref[pl.ds(r0, chunk), cols] = y
            u = collapse(ux)
            upd = _dot_tn(u, b_e) + _dot_tn(v_c, k_e)
            state_ref[g] = s0 * p_end + jnp.where(state_mask, upd, 0.0)
        return 0

    lax.fori_loop(0, n_chunks, chunk_step, 0)

    y = y_ref[...]
    head_avg = (head_ones.astype(F32) * (1.0 / HEAD_DIM)).astype(BF16)
    mu_y = _dot_sel_rhs(y, head_avg)
    d = y - mu_y
    var_y = _dot_sel_rhs(d * d, head_avg)
    yn = d * lax.rsqrt(var_y + GN_EPS) * gnw_ref[...] + gnb_ref[...]
    bonus = _dot_sel_rhs(r * k2 * rk_ref[...], head_ones)
    o_ref[0] = (yn + bonus * v) * gate

    @pl.when(t == pl.num_programs(1) - 1)
    def _():
        pick = jnp.where(_iota((gw, HEAD_DIM), 0) % HEAD_DIM == _iota((gw, HEAD_DIM), 1),
                         1.0, 0.0).astype(BF16)
        for g in range(n_groups):
            wkv_ref[0, g * gw:(g + 1) * gw, :] = _dot_sel_rhs(state_ref[g], pick)


def _rwkv(p_rwkv, shift0, wkv0, prm):
    bsz, t, _ = p_rwkv.shape
    if t >= RWKV_CHUNK:
        chunk, heads_per_group, tb = RWKV_CHUNK, 4, min(t, ROW_TILE)
    else:
        chunk, heads_per_group, tb = t, N_HEADS, t
    gw = heads_per_group * HEAD_DIM
    n_groups = N_HEADS // heads_per_group
    row = lambda width: pl.BlockSpec((1, tb, width), lambda b, i: (b, i, 0))
    per_b = lambda shape: pl.BlockSpec((1,) + shape, lambda b, i: (b, 0, 0))
    vec = lambda width: _const_spec((1, width))
    o, wkv, shift = pl.pallas_call(
        functools.partial(_rwkv_kernel, chunk=chunk, heads_per_group=heads_per_group, tb=tb),
        grid=(bsz, t // tb),
        in_specs=[row(C_RWKV), per_b((1, C_RWKV)), per_b((WIDTH, HEAD_DIM)),
                  vec(C_RWKV), vec(WIDTH), _const_spec((LANES, 2 * WIDTH)), vec(WIDTH),
                  _const_spec((R_GATE, WIDTH)), vec(WIDTH), vec(WIDTH), vec(WIDTH), vec(WIDTH),
                  vec(WIDTH)],
        out_specs=[row(WIDTH), per_b((WIDTH, HEAD_DIM)), per_b((1, C_RWKV))],
        out_shape=[jax.ShapeDtypeStruct((bsz, t, WIDTH), F32),
                   jax.ShapeDtypeStruct((bsz, WIDTH, HEAD_DIM), F32),
                   jax.ShapeDtypeStruct((bsz, 1, C_RWKV), F32)],
        scratch_shapes=[pltpu.VMEM((n_groups, gw, gw), F32), pltpu.VMEM((1, C_RWKV), F32),
                        pltpu.VMEM((8, tb, WIDTH), F32), pltpu.VMEM((tb, WIDTH), F32)],
        compiler_params=_params("arbitrary", "arbitrary"),
        name="rwkv7",
    )(p_rwkv, shift0.reshape(bsz, 1, C_RWKV), wkv0.reshape(bsz, WIDTH, HEAD_DIM),
      prm["mu"], prm["w0"], prm["lora"], prm["a0"], prm["gup"], prm["kk"], prm["ka"], prm["rk"],
      prm["gnw"], prm["gnb"])
    return (o, wkv.reshape(bsz, N_HEADS, HEAD_DIM, HEAD_DIM), shift.reshape(bsz, C_RWKV))


def _merge_kernel(x_ref, oa_ref, ob_ref, ga_ref, gb_ref, wa_ref, wb_ref, wo_ref, lnw_ref, lnb_ref,
                  y_ref, *, alpha):
    merged = ga_ref[0] * _dot(oa_ref[0], wa_ref[...]) + gb_ref[0] * _dot(ob_ref[0], wb_ref[...])
    y = alpha * x_ref[0] + _dot(merged, wo_ref[...])
    y_ref[0] = _layer_norm(y, lnw_ref[...], lnb_ref[...])


def _merge(x, o_a, o_b, g_a, g_b, w_a, w_b, w_o, ln_w, ln_b, alpha):
    bsz, t, d = x.shape
    tm = min(t, ROW_TILE)
    row = lambda width: pl.BlockSpec((1, tm, width), lambda b, i: (b, i, 0))
    return pl.pallas_call(
        functools.partial(_merge_kernel, alpha=alpha),
        grid=(bsz, t // tm),
        in_specs=[row(d), row(WIDTH), row(WIDTH), row(d), row(d),
                  _const_spec(w_a.shape), _const_spec(w_b.shape), _const_spec(w_o.shape),
                  _const_spec((1, d)), _const_spec((1, d))],
        out_specs=row(d),
        out_shape=jax.ShapeDtypeStruct((bsz, t, d), F32),
        compiler_params=_params("arbitrary", "arbitrary"),
        name="merge_ln",
    )(x, o_a, o_b, g_a, g_b, w_a, w_b, w_o, ln_w, ln_b)


def _gelu_tanh(x):
    return x * (0.5 * (1.0 + jnp.tanh(math.sqrt(2.0 / math.pi) * (x + 0.044715 * (x * x * x)))))


def _ffn_kernel(x_ref, conv0_ref, wup_ref, cw_ref, cb_ref, wdn_ref, lnw_ref, lnb_ref,
                y_ref, conv_ref, tail_ref, *, alpha, tm):
    t = pl.program_id(1)
    d_ff = cb_ref.shape[-1]
    n_tail = CONV_W - 1

    @pl.when(t == 0)
    def _():
        tail_ref[...] = conv0_ref[0]

    x = x_ref[0]
    xb = x.astype(BF16)
    u = jnp.dot(xb, wup_ref[:, :d_ff], preferred_element_type=F32)
    gate = jnp.dot(xb, wup_ref[:, d_ff:], preferred_element_type=F32)
    rowi = _iota((tm, d_ff), 0)
    conv = cb_ref[...] + cw_ref[n_tail:n_tail + 1, :] * u
    for back in range(1, CONV_W):
        prev = pltpu.roll(u, back, 0)
        for j in range(back):
            prev = jnp.where(rowi == j, tail_ref[n_tail - back + j:n_tail - back + j + 1, :], prev)
        conv = conv + cw_ref[n_tail - back:n_tail - back + 1, :] * prev
    tail_ref[...] = u[tm - n_tail:tm, :]
    conv_ref[0] = u[tm - n_tail:tm, :]
    hmid = _gelu_tanh(conv) * gate
    y = alpha * x + _dot(hmid, wdn_ref[...])
    y_ref[0] = _layer_norm(y, lnw_ref[...], lnb_ref[...])


def _ffn(x, conv0, w_up, conv_w, conv_b, w_down, ln_w, ln_b, alpha):
    bsz, t, d = x.shape
    d_ff = w_down.shape[0]
    tm = min(t, ROW_TILE)
    assert tm >= CONV_W - 1
    row = lambda width: pl.BlockSpec((1, tm, width), lambda b, i: (b, i, 0))
    per_b = lambda shape: pl.BlockSpec((1,) + shape, lambda b, i: (b, 0, 0))
    return pl.pallas_call(
        functools.partial(_ffn_kernel, alpha=alpha, tm=tm),
        grid=(bsz, t // tm),
        in_specs=[row(d), per_b((CONV_W - 1, d_ff)), _const_spec(w_up.shape),
                  _const_spec((CONV_W, d_ff)), _const_spec((1, d_ff)), _const_spec(w_down.shape),
                  _const_spec((1, d)), _const_spec((1, d))],
        out_specs=[row(d), per_b((CONV_W - 1, d_ff))],
        out_shape=[jax.ShapeDtypeStruct((bsz, t, d), F32),
                   jax.ShapeDtypeStruct((bsz, CONV_W - 1, d_ff), F32)],
        scratch_shapes=[pltpu.VMEM((CONV_W - 1, d_ff), F32)],
        compiler_params=_params("arbitrary", "arbitrary"),
        name="conv_ffn_ln",
    )(x, conv0, w_up, conv_w, conv_b, w_down, ln_w, ln_b)


def _layer_params(l, w_in, b_forget, mu_shift, decay_w0, decay_up, iclr_a0, iclr_up, gate_up,
                  k_k, k_a, r_k, gn_w, gn_b, w_branch_a, w_branch_b, w_out, ln1_w, ln1_b,
                  w_up, conv_w, conv_b, w_down, ln2_w, ln2_b):
    d = w_in.shape[1]
    c_fox = 3 * WIDTH + N_HEADS
    w = w_in[l]
    w_pad = jnp.concatenate([w[:, :c_fox], jnp.zeros((d, LANES - N_HEADS), F32), w[:, c_fox:]], axis=1)
    bf = jnp.zeros((1, LANES), F32).at[0, :N_HEADS].set(b_forget[l])
    zeros = jnp.zeros((R_DECAY, WIDTH), F32)
    lora = jnp.concatenate([jnp.concatenate([decay_up[l], zeros], axis=1),
                            jnp.concatenate([zeros, iclr_up[l]], axis=1)], axis=0)
    vec = lambda z: z.reshape(1, -1)
    return dict(
        w_in=w_pad.astype(BF16), bf=bf,
        rwkv=dict(mu=vec(mu_shift[l]), w0=vec(decay_w0[l]), lora=lora.astype(BF16), a0=vec(iclr_a0[l]),
                  gup=gate_up[l].astype(BF16), kk=vec(k_k[l]), ka=vec(k_a[l]), rk=vec(r_k[l]),
                  gnw=vec(gn_w[l]), gnb=vec(gn_b[l])),
        w_a=w_branch_a[l].astype(BF16), w_b=w_branch_b[l].astype(BF16), w_o=w_out[l].astype(BF16),
        ln1_w=vec(ln1_w[l]), ln1_b=vec(ln1_b[l]),
        w_up=w_up[l].astype(BF16), conv_w=conv_w[l], conv_b=vec(conv_b[l]), w_down=w_down[l].astype(BF16),
        ln2_w=vec(ln2_w[l]), ln2_b=vec(ln2_b[l]))


def _run_group(x, layers, attend, shift0, wkv0, conv0, alpha, emit_bf):
    bsz, t, _ = x.shape
    ks, vs, lfs, wkvs, shifts, convs = [], [], [], [], [], []
    for l, prm in enumerate(layers):
        outs = _in_proj(x, prm["w_in"], prm["bf"], emit_bf=emit_bf)
        if emit_bf:
            q_bf, k, v, k_bf, vt_bf, logf, c, p_rwkv, g_a, g_b = outs
            o_a = attend(l, q_bf, k_bf, vt_bf, c)
        else:
            q_bf, k, v, logf, c, p_rwkv, g_a, g_b = outs
            o_a = attend(l, q_bf, k, v, c)
        o_b, wkv_l, shift_l = _rwkv(p_rwkv, shift0[l], wkv0[l], prm["rwkv"])
        x = _merge(x, o_a, o_b, g_a, g_b, prm["w_a"], prm["w_b"], prm["w_o"],
                   prm["ln1_w"], prm["ln1_b"], alpha)
        x, conv_l = _ffn(x, conv0[l], prm["w_up"], prm["conv_w"], prm["conv_b"], prm["w_down"],
                         prm["ln2_w"], prm["ln2_b"], alpha)
        ks.append(k.reshape(bsz, t, N_HEADS, HEAD_DIM))
        vs.append(v.reshape(bsz, t, N_HEADS, HEAD_DIM))
        lfs.append(logf)
        wkvs.append(wkv_l)
        shifts.append(shift_l)
        convs.append(conv_l)
    return (x, jnp.stack(ks), jnp.stack(vs), jnp.stack(lfs), jnp.stack(wkvs),
            jnp.stack(shifts), jnp.stack(convs))


def kernel(x_prompt, x_sample, cache_k, cache_v, cache_logf, state_wkv, state_shift, state_conv, page_table, w_in, b_forget, mu_shift, decay_w0, decay_up, iclr_a0, iclr_up, gate_up, k_k, k_a, r_k, gn_w, gn_b, w_branch_a, w_branch_b, w_out, ln1_w, ln1_b, w_up, conv_w, conv_b, w_down, ln2_w, ln2_b):
    depth = w_in.shape[0]
    alpha = (2.0 * depth) ** 0.25
    layers = [_layer_params(l, w_in, b_forget, mu_shift, decay_w0, decay_up, iclr_a0, iclr_up,
                            gate_up, k_k, k_a, r_k, gn_w, gn_b, w_branch_a, w_branch_b, w_out,
                            ln1_w, ln1_b, w_up, conv_w, conv_b, w_down, ln2_w, ln2_b)
              for l in range(depth)]

    n_pool = cache_k.shape[1]
    flat_k = cache_k.reshape(depth * n_pool, PAGE_SIZE, WIDTH)
    flat_v = cache_v.reshape(depth * n_pool, PAGE_SIZE, WIDTH)
    flat_lf = cache_logf.reshape(depth * n_pool, PAGE_SIZE, N_HEADS)

    def attend_prompt(l, q_bf, k_bf, vt_bf, c):
        return _fox_prompt(q_bf, k_bf, vt_bf, c)

    def attend_sample(l, q_bf, k, v, c):
        return _fox_cached(q_bf, k, v, c, flat_k, flat_v, flat_lf, page_table + l * n_pool)

    bp, _, d = x_prompt.shape
    d_ff = w_down.shape[1]
    prompt = _run_group(x_prompt, layers, attend_prompt,
                        jnp.zeros((depth, bp, C_RWKV), F32),
                        jnp.zeros((depth, bp, N_HEADS, HEAD_DIM, HEAD_DIM), F32),
                        jnp.zeros((depth, bp, CONV_W - 1, d_ff), F32), alpha, True)
    sample = _run_group(x_sample, layers, attend_sample, state_shift, state_wkv, state_conv,
                        alpha, False)
    (y_p, k_p, v_p, lf_p, wkv_p, shift_p, conv_p) = prompt
    (y_s, k_s, v_s, lf_s, wkv_s, shift_s, conv_s) = sample
    return (y_p, y_s, k_p, v_p, lf_p, wkv_p, shift_p, conv_p,
            k_s, v_s, lf_s, wkv_s, shift_s, conv_s)
```

```python
import functools
import math

import jax
import jax.numpy as jnp
import numpy as np
from jax import lax
from jax.experimental import pallas as pl
from jax.experimental.pallas import tpu as pltpu

F32 = jnp.float32
BF16 = jnp.bfloat16

N_HEADS = 8
HEAD_DIM = 64
WIDTH = N_HEADS * HEAD_DIM
R_DECAY = 64
R_ICLR = 64
R_GATE = 128
C_RWKV = 3 * WIDTH + R_DECAY + R_ICLR + R_GATE
CONV_W = 3
PAGE_SIZE = 128
LN_EPS = 1e-5
GN_EPS = 64e-5
NEG_BIG = -1e30

LOG2E = math.log2(math.e)

LANES = 128
ROW_TILE = 256
ATTN_TILE = 512
ATTN_SUB = 256
V_ROWS = HEAD_DIM + 16
RWKV_CHUNK = 64
PAGES_PER_STEP = 16
VMEM_LIMIT_BYTES = 56 * 1024 * 1024

_OFF_Q, _OFF_K, _OFF_V, _OFF_F = 0, WIDTH, 2 * WIDTH, 3 * WIDTH
_OFF_RWKV = 3 * WIDTH + LANES
_OFF_GA = _OFF_RWKV + C_RWKV


def _iota(shape, dim):
    return lax.broadcasted_iota(jnp.int32, shape, dim)


def _dot(a, b):
    return jnp.dot(a.astype(BF16), b.astype(BF16), preferred_element_type=F32)


def _dot_nt(a, b):
    return lax.dot_general(a.astype(BF16), b.astype(BF16), (((1,), (1,)), ((), ())),
                           preferred_element_type=F32)


def _dot_tn(a, b):
    return lax.dot_general(a.astype(BF16), b.astype(BF16), (((0,), (0,)), ((), ())),
                           preferred_element_type=F32)


def _split3(x):
    hi = x.astype(BF16)
    r1 = x - hi.astype(F32)
    mid = r1.astype(BF16)
    lo = (r1 - mid.astype(F32)).astype(BF16)
    return hi, mid, lo


def _dot_sel_rhs(x, sel, pieces=3):
    hi, mid, lo = _split3(x)
    d = lambda p: jnp.dot(p, sel, preferred_element_type=F32)
    if pieces == 2:
        return d(hi) + d(mid)
    return d(hi) + (d(mid) + d(lo))


def _dot_sel_lhs(sel, x):
    hi, mid, lo = _split3(x)
    d = lambda p: jnp.dot(sel, p, preferred_element_type=F32)
    return d(hi) + (d(mid) + d(lo))


def _sigmoid(x):
    return 1.0 / (1.0 + jnp.exp(-x))


def _log_sigmoid(x):
    return jnp.minimum(x, 0.0) - jnp.log1p(jnp.exp(-jnp.abs(x)))


def _layer_norm(x, w, b):
    mu = jnp.mean(x, axis=-1, keepdims=True)
    d = x - mu
    var = jnp.mean(d * d, axis=-1, keepdims=True)
    return d * lax.rsqrt(var + LN_EPS) * w + b


def _params(*sem):
    return pltpu.CompilerParams(dimension_semantics=sem, vmem_limit_bytes=VMEM_LIMIT_BYTES)


def _const_spec(shape):
    zeros = (0,) * len(shape)
    return pl.BlockSpec(shape, lambda *_: zeros, pipeline_mode=pl.Buffered(1))


def _aug_constants():
    place_q = np.zeros((3 * LANES, N_HEADS * LANES), np.float32)
    place_k = np.zeros((3 * LANES, N_HEADS * LANES), np.float32)
    ones_q = np.zeros((1, N_HEADS * LANES), np.float32)
    ones_k = np.zeros((1, N_HEADS * LANES), np.float32)
    for h in range(N_HEADS):
        base = LANES * h + (HEAD_DIM if h % 2 == 0 else 0)
        for piece in range(3):
            place_q[piece * LANES + h, base + piece] = 1.0
            place_k[piece * LANES + h, base + 3 + piece] = -1.0
            ones_q[0, base + 3 + piece] = 1.0
            ones_k[0, base + piece] = 1.0
    return (jnp.asarray(place_q, BF16), jnp.asarray(place_k, BF16),
            jnp.asarray(ones_q), jnp.asarray(ones_k))


def _in_proj_kernel(x_ref, w_ref, bf_ref, *refs, tm, augment):
    if augment:
        (pq_ref, pk_ref, oq_ref, ok_ref,
         q_ref, k_ref, v_ref, kaug_ref, vt_ref, logf_ref, c_ref, prw_ref, ga_ref, gb_ref, carry_ref) = refs
    else:
        (q_ref, k_ref, v_ref, logf_ref, c_ref, prw_ref, ga_ref, gb_ref, carry_ref) = refs
    t = pl.program_id(1)
    x = x_ref[0].astype(BF16)

    def proj(lo, hi):
        return jnp.dot(x, w_ref[:, lo:hi], preferred_element_type=F32)

    q = proj(_OFF_Q, _OFF_Q + WIDTH)
    k = proj(_OFF_K, _OFF_K + WIDTH)
    k_ref[0] = k
    v = proj(_OFF_V, _OFF_V + WIDTH)
    v_ref[0] = v
    if augment:
        vt = v.T.astype(BF16)
        extra = jnp.where(_iota((V_ROWS - HEAD_DIM, tm), 0) == 0, 1.0, 0.0).astype(BF16)
        vt_ref[0, 0] = jnp.concatenate(
            [z for h in range(N_HEADS) for z in (vt[HEAD_DIM * h:HEAD_DIM * (h + 1)], extra)], axis=0)
    else:
        q_ref[0] = q * HEAD_DIM ** -0.5

    z = proj(_OFF_F, _OFF_F + LANES) + bf_ref[...]
    lane = _iota((tm, LANES), 1)
    lf = jnp.where(lane < N_HEADS, _log_sigmoid(z), 0.0)
    logf_ref[0] = lf[:, :N_HEADS]

    @pl.when(t == 0)
    def _():
        carry_ref[...] = jnp.zeros_like(carry_ref)

    tril = jnp.where(_iota((tm, tm), 1) <= _iota((tm, tm), 0), 1.0, 0.0).astype(BF16)
    cs = _dot_sel_lhs(tril, lf) + carry_ref[...]
    carry_ref[...] = cs[tm - 1:tm, :]
    c_ref[0] = cs[:, :N_HEADS]

    if augment:
        pieces = jnp.concatenate(_split3(cs * LOG2E), axis=1)
        head_lane = _iota((tm, N_HEADS * LANES), 1)
        own = (head_lane % LANES) // HEAD_DIM == (head_lane // LANES) % 2
        pair = lambda z: jnp.concatenate(
            [z[:, LANES * (h // 2):LANES * (h // 2 + 1)] for h in range(N_HEADS)], axis=1)
        aug_q = jnp.dot(pieces, pq_ref[...], preferred_element_type=F32) + oq_ref[...]
        aug_k = jnp.dot(pieces, pk_ref[...], preferred_element_type=F32) + ok_ref[...]
        q_ref[0] = jnp.where(own, pair(q * (HEAD_DIM ** -0.5 * LOG2E)), aug_q).astype(BF16)
        kaug_ref[0] = jnp.where(own, pair(k), aug_k).astype(BF16)

    prw_ref[0] = proj(_OFF_RWKV, _OFF_RWKV + C_RWKV)
    d_model = ga_ref.shape[-1]
    ga_ref[0] = _sigmoid(proj(_OFF_GA, _OFF_GA + d_model))
    gb_ref[0] = _sigmoid(proj(_OFF_GA + d_model, _OFF_GA + 2 * d_model))


def _in_proj(x, w, bf, *, augment):
    bsz, t, d = x.shape
    tm = min(t, ROW_TILE)
    nt = t // tm
    c_all = w.shape[1]
    row = lambda width: pl.BlockSpec((1, tm, width), lambda b, i: (b, i, 0))
    in_specs = [row(d), _const_spec((d, c_all)), _const_spec((1, LANES))]
    operands = [x, w, bf]
    wide = N_HEADS * LANES
    if augment:
        per_tile = ATTN_TILE // tm
        consts = _aug_constants()
        in_specs += [_const_spec(c.shape) for c in consts]
        operands += list(consts)
        out_shape = [jax.ShapeDtypeStruct((bsz, t, wide), BF16),
                     jax.ShapeDtypeStruct((bsz, t, WIDTH), F32),
                     jax.ShapeDtypeStruct((bsz, t, WIDTH), F32),
                     jax.ShapeDtypeStruct((bsz, t, wide), BF16),
                     jax.ShapeDtypeStruct((bsz, t // ATTN_TILE, N_HEADS * V_ROWS, ATTN_TILE), BF16)]
        out_specs = [row(wide), row(WIDTH), row(WIDTH), row(wide),
                     pl.BlockSpec((1, 1, N_HEADS * V_ROWS, tm),
                                  lambda b, i: (b, i // per_tile, 0, i % per_tile))]
    else:
        out_shape = [jax.ShapeDtypeStruct((bsz, t, WIDTH), F32),
                     jax.ShapeDtypeStruct((bsz, t, WIDTH), F32),
                     jax.ShapeDtypeStruct((bsz, t, WIDTH), F32)]
        out_specs = [row(WIDTH), row(WIDTH), row(WIDTH)]
    out_shape += [jax.ShapeDtypeStruct((bsz, t, N_HEADS), F32),
                  jax.ShapeDtypeStruct((bsz, t, N_HEADS), F32),
                  jax.ShapeDtypeStruct((bsz, t, C_RWKV), F32),
                  jax.ShapeDtypeStruct((bsz, t, d), F32),
                  jax.ShapeDtypeStruct((bsz, t, d), F32)]
    out_specs += [row(N_HEADS), row(N_HEADS), row(C_RWKV), row(d), row(d)]
    return pl.pallas_call(
        functools.partial(_in_proj_kernel, tm=tm, augment=augment),
        grid=(bsz, nt),
        in_specs=in_specs,
        out_specs=out_specs,
        out_shape=out_shape,
        scratch_shapes=[pltpu.VMEM((1, LANES), F32)],
        compiler_params=_params("arbitrary", "arbitrary"),
        name="in_proj",
    )(*operands)


def _fox_prompt_kernel(q_ref, k_ref, vt_ref, o_ref, *, tile):
    i = pl.program_id(2)
    heads = (0, 1)
    key_pos = _iota((ATTN_SUB, tile), 0)
    query_pos = _iota((ATTN_SUB, tile), 1)

    def step(j, carry, masked):
        start = pl.multiple_of(j * tile, tile)
        carry = list(carry)
        subs = range(tile // ATTN_SUB)
        scores = {}
        for sub in subs:
            keys = pl.ds(start + sub * ATTN_SUB, ATTN_SUB)
            for h in heads:
                lanes = slice(LANES * h, LANES * (h + 1))
                s = _dot_nt(k_ref[0, keys, lanes], q_ref[0, :, lanes])
                if masked:
                    s = jnp.where(key_pos + sub * ATTN_SUB <= query_pos, s, NEG_BIG)
                scores[sub, h] = s
        probs = {}
        scale = {}
        for sub in subs:
            for h in heads:
                m, acc = carry[h]
                m_new = jnp.maximum(m, jnp.max(scores[sub, h], axis=0, keepdims=True))
                probs[sub, h] = jnp.exp2(scores[sub, h] - m_new).astype(BF16)
                scale[sub, h] = jnp.exp2(m - m_new)
                carry[h] = (m_new, acc)
        for sub in subs:
            cols = slice(sub * ATTN_SUB, (sub + 1) * ATTN_SUB)
            for h in heads:
                m_new, acc = carry[h]
                vt = vt_ref[0, j, V_ROWS * h:V_ROWS * (h + 1), cols]
                pv = jnp.dot(vt, probs[sub, h], preferred_element_type=F32)
                carry[h] = (m_new, scale[sub, h] * acc + pv)
        return tuple(carry)

    init = tuple((jnp.full((1, tile), NEG_BIG, F32), jnp.zeros((V_ROWS, tile), F32)) for _ in heads)
    carry = lax.fori_loop(0, i, lambda j, c: step(j, c, False), init)
    carry = step(i, carry, True)
    o_ref[0] = jnp.concatenate([acc[:HEAD_DIM] / acc[HEAD_DIM:HEAD_DIM + 1] for _, acc in carry],
                               axis=0).T


def _fox_prompt(q_aug, k_aug, vt_bf):
    bsz, t, _ = q_aug.shape
    tile = vt_bf.shape[-1]
    nq = t // tile
    n_pairs = N_HEADS // 2
    return pl.pallas_call(
        functools.partial(_fox_prompt_kernel, tile=tile),
        grid=(bsz, n_pairs, nq),
        in_specs=[pl.BlockSpec((1, tile, 2 * LANES), lambda b, hp, i: (b, i, hp)),
                  pl.BlockSpec((1, t, 2 * LANES), lambda b, hp, i: (b, 0, hp)),
                  pl.BlockSpec((1, nq, 2 * V_ROWS, tile), lambda b, hp, i: (b, 0, hp, 0))],
        out_specs=pl.BlockSpec((1, tile, LANES), lambda b, hp, i: (b, i, hp)),
        out_shape=jax.ShapeDtypeStruct((bsz, t, WIDTH), F32),
        compiler_params=_params("arbitrary", "arbitrary", "arbitrary"),
        name="fox_prompt",
    )(q_aug, k_aug, vt_bf)


def _fox_cached_kernel(pt_ref, q_ref, knew_ref, vnew_ref, cq_ref, ck_ref, later_ref, *refs, n_group):
    del pt_ref
    kp = refs[:n_group]
    vp = refs[n_group:2 * n_group]
    lp = refs[2 * n_group:3 * n_group]
    o_ref, m_ref, l_ref, acc_ref, carry_ref, both_ref = refs[3 * n_group:]
    step = pl.program_id(1)
    t_new = q_ref.shape[1]
    rows = N_HEADS * t_new
    own_head = _iota((rows, WIDTH), 0) // t_new == _iota((rows, WIDTH), 1) // HEAD_DIM
    qx = jnp.where(own_head, jnp.concatenate([q_ref[0]] * N_HEADS, axis=0), 0.0).astype(BF16)

    @pl.when(step == 0)
    def _():
        s = _dot_nt(qx, knew_ref[0])
        s = s + (cq_ref[0] - ck_ref[0])
        qpos = _iota((rows, t_new), 0) % t_new
        s = jnp.where(_iota((rows, t_new), 1) <= qpos, s, NEG_BIG)
        m = jnp.max(s, axis=1, keepdims=True)
        p = jnp.exp(s - m)
        m_ref[...] = m
        l_ref[...] = jnp.sum(p, axis=1, keepdims=True)
        acc_ref[...] = _dot(p, vnew_ref[0])
        carry_ref[...] = jnp.broadcast_to(cq_ref[0], carry_ref.shape)

    groups = range(n_group)
    raw = [_dot(qx, kp[g][0]) for g in groups]
    lf_t = jnp.concatenate([lp[g][0] for g in groups], axis=0)
    both_ref[...] = _dot_sel_rhs(lf_t, later_ref[...])
    carry = carry_ref[...]
    scores = []
    for g in groups:
        both = jnp.concatenate(
            [jnp.broadcast_to(both_ref[g * N_HEADS + h:g * N_HEADS + h + 1, :], (t_new, 2 * PAGE_SIZE))
             for h in range(N_HEADS)], axis=0)
        scores.append(raw[g] + (both[:, :PAGE_SIZE] + carry))
        carry = carry + both[:, PAGE_SIZE:]
    carry_ref[...] = carry

    smax = scores[0]
    for s in scores[1:]:
        smax = jnp.maximum(smax, s)
    m_old = m_ref[...]
    m_new = jnp.maximum(m_old, jnp.max(smax, axis=1, keepdims=True))
    alpha = jnp.exp(m_old - m_new)
    l = alpha * l_ref[...]
    acc = alpha * acc_ref[...]
    probs = [jnp.exp(scores[g] - m_new) for g in groups]
    for g in groups:
        l = l + jnp.sum(probs[g], axis=1, keepdims=True)
        acc = acc + _dot_nt(probs[g], vp[g][0])
    m_ref[...] = m_new
    l_ref[...] = l
    acc_ref[...] = acc

    @pl.when(step == pl.num_programs(1) - 1)
    def _():
        o = jnp.where(own_head, acc / l, 0.0)
        out = o[0:t_new]
        for h in range(1, N_HEADS):
            out = out + o[h * t_new:(h + 1) * t_new]
        o_ref[0] = out


def _fox_cached(q_bf, k_new, v_new, c_new, cache_k, cache_v, cache_logf, pages):
    bsz, t_new, _ = q_bf.shape
    n_pages = pages.shape[1]
    n_group = math.gcd(PAGES_PER_STEP, n_pages)
    n_steps = n_pages // n_group
    rows = N_HEADS * t_new
    c_q = c_new.transpose(0, 2, 1).reshape(bsz, rows, 1)
    c_k = jnp.repeat(c_new.transpose(0, 2, 1), t_new, axis=1)

    def page_map(g, n_trailing):
        zeros = (0,) * n_trailing
        return lambda b, s, pt: (pt[b, n_pages - 1 - (s * n_group + g)],) + zeros

    def per_b(shape):
        zeros = (0,) * len(shape)
        return pl.BlockSpec((1,) + shape, lambda b, s, pt: (b,) + zeros)

    keys = np.arange(PAGE_SIZE)
    later = np.concatenate([(keys[:, None] > keys[None, :]).astype(np.float32),
                            np.ones((PAGE_SIZE, PAGE_SIZE), np.float32)], axis=1)
    later = jnp.asarray(later, BF16)

    tok_blk = (t_new, WIDTH)
    in_specs = [per_b(tok_blk), per_b(tok_blk), per_b(tok_blk), per_b((rows, 1)), per_b((rows, t_new)),
                pl.BlockSpec(later.shape, lambda b, s, pt: (0, 0), pipeline_mode=pl.Buffered(1))]
    kv_blk = (1, WIDTH, PAGE_SIZE)
    in_specs += [pl.BlockSpec(kv_blk, page_map(g, 2)) for g in range(n_group)]
    in_specs += [pl.BlockSpec(kv_blk, page_map(g, 2)) for g in range(n_group)]
    in_specs += [pl.BlockSpec((1, N_HEADS, PAGE_SIZE), page_map(g, 2)) for g in range(n_group)]
    return pl.pallas_call(
        functools.partial(_fox_cached_kernel, n_group=n_group),
        grid_spec=pltpu.PrefetchScalarGridSpec(
            num_scalar_prefetch=1,
            grid=(bsz, n_steps),
            in_specs=in_specs,
            out_specs=per_b(tok_blk),
            scratch_shapes=[pltpu.VMEM((rows, 1), F32), pltpu.VMEM((rows, 1), F32),
                            pltpu.VMEM((rows, WIDTH), F32), pltpu.VMEM((rows, PAGE_SIZE), F32),
                            pltpu.VMEM((n_group * N_HEADS, later.shape[1]), F32)]),
        out_shape=jax.ShapeDtypeStruct((bsz,) + tok_blk, F32),
        compiler_params=_params("arbitrary", "arbitrary"),
        name="fox_cached",
    )(pages, q_bf, k_new, v_new, c_q, c_k, later,
      *([cache_k] * n_group), *([cache_v] * n_group), *([cache_logf] * n_group))


def _unit_lower_inverse(n_mats, eye, blk_of_row, blk_of_col, chunk):
    base = 8
    same = lambda size: (blk_of_row // (size // base)) == (blk_of_col // (size // base))
    ms = [jnp.where(same(base), n, 0.0) for n in n_mats]
    xs = [eye + m for m in ms]
    for _ in range(2):
        ms = [_dot(m, m) for m in ms]
        xs = [x + _dot(x, m) for x, m in zip(xs, ms)]
    size = base
    while size < chunk:
        ring = same(2 * size) & jnp.logical_not(same(size))
        half = [_dot(x, jnp.where(ring, n, 0.0)) for x, n in zip(xs, n_mats)]
        xs = [x + _dot(h, x) for x, h in zip(xs, half)]
        size *= 2
    return xs


def _rwkv_kernel(p_ref, shift0_ref, wkv0_ref, mu_ref, w0_ref, lora_ref, a0_ref, gup_ref,
                 kk_ref, ka_ref, rk_ref, gnw_ref, gnb_ref,
                 o_ref, wkv_ref, shift_ref,
                 state_ref, prev_ref, pre_ref, y_ref, *, chunk, heads_per_group, tb):
    t = pl.program_id(1)
    gw = heads_per_group * HEAD_DIM
    n_groups = N_HEADS // heads_per_group
    gl = heads_per_group * chunk
    n_chunks = tb // chunk

    row_head = _iota((gw, gw), 0) // HEAD_DIM
    col_head = _iota((gw, gw), 1) // HEAD_DIM
    state_mask = row_head == col_head

    @pl.when(t == 0)
    def _():
        prev_ref[...] = shift0_ref[0]
        tile_lanes = jnp.where(_iota((HEAD_DIM, gw), 0) == _iota((HEAD_DIM, gw), 1) % HEAD_DIM,
                               1.0, 0.0).astype(BF16)
        for g in range(n_groups):
            s0 = wkv0_ref[0, g * gw:(g + 1) * gw, :]
            state_ref[g] = jnp.where(state_mask, _dot_sel_rhs(s0, tile_lanes), 0.0)

    p = p_ref[0]
    first_row = _iota((tb, C_RWKV), 0) == 0
    shifted = jnp.where(first_row, prev_ref[...], pltpu.roll(p, 1, 0))
    prev_ref[...] = p[tb - 1:tb, :]
    shift_ref[0] = p[tb - 1:tb, :]
    xs = p + mu_ref[...] * (shifted - p)
    r = xs[:, 0:WIDTH]
    k = xs[:, WIDTH:2 * WIDTH]
    v = xs[:, 2 * WIDTH:3 * WIDTH]
    wa = xs[:, 3 * WIDTH:3 * WIDTH + LANES]
    gd = xs[:, 3 * WIDTH + LANES:]
    lane = _iota((tb, LANES), 1)
    lora = _dot(jnp.where(lane < R_DECAY, jnp.tanh(wa), wa), lora_ref[...])
    w_raw = _log_sigmoid(w0_ref[...] + lora[:, :WIDTH]) - 0.5
    lw = -jnp.exp(w_raw)
    iclr = _sigmoid(a0_ref[...] + lora[:, WIDTH:])
    gate = _dot(_sigmoid(gd), gup_ref[...])

    seg_r = _iota((WIDTH, WIDTH), 0) // HEAD_DIM
    seg_c = _iota((WIDTH, WIDTH), 1) // HEAD_DIM
    head_ones = jnp.where(seg_r == seg_c, 1.0, 0.0).astype(BF16)

    kk = k * kk_ref[...]
    kk_norm = jnp.sqrt(_dot_sel_rhs(kk * kk, head_ones, pieces=2))
    kk = kk / jnp.maximum(kk_norm, 1e-12)
    k2 = k * (1.0 + (iclr - 1.0) * ka_ref[...])
    a = -kk
    b = kk * iclr

    blk_r = _iota((tb, tb), 0) // chunk
    blk_c = _iota((tb, tb), 1) // chunk
    same_chunk = blk_r == blk_c
    incl = jnp.where(same_chunk & (_iota((tb, tb), 1) <= _iota((tb, tb), 0)), 1.0, 0.0).astype(BF16)
    cum = _dot_sel_lhs(incl, lw)
    cum_end = jnp.concatenate(
        [jnp.broadcast_to(cum[(c + 1) * chunk - 1:(c + 1) * chunk, :], (chunk, WIDTH))
         for c in range(n_chunks)], axis=0)
    pre_ref[0] = a * jnp.exp(cum - lw)
    pre_ref[1] = r * jnp.exp(cum)
    pre_ref[2] = b * jnp.exp(-cum)
    pre_ref[3] = k2 * jnp.exp(-cum)
    pre_ref[4] = b * jnp.exp(cum_end - cum)
    pre_ref[5] = k2 * jnp.exp(cum_end - cum)
    pre_ref[6] = v
    pre_ref[7] = jnp.exp(cum_end)

    e_row_head = _iota((gl, gw), 0) // chunk
    e_col_head = _iota((gl, gw), 1) // HEAD_DIM
    expand_mask = e_row_head == e_col_head
    sq_r = _iota((gl, gl), 0)
    sq_c = _iota((gl, gl), 1)
    same_head = (sq_r // chunk) == (sq_c // chunk)
    strict = same_head & (sq_c < sq_r)
    lower = same_head & (sq_c <= sq_r)
    eye = jnp.where(sq_r == sq_c, 1.0, 0.0)
    blk8_r = sq_r // 8
    blk8_c = sq_c // 8

    def expand(x):
        return jnp.where(expand_mask, jnp.concatenate([x] * heads_per_group, axis=0), 0.0)

    def collapse(x):
        out = x[0:chunk]
        for h in range(1, heads_per_group):
            out = out + x[h * chunk:(h + 1) * chunk]
        return out

    def tile_rows(x):
        return jnp.concatenate([x] * heads_per_group, axis=0)

    pairs = [(c, g) for c in range(n_chunks) for g in range(n_groups)]

    def ld(idx, c, g, n_rows=chunk):
        return pre_ref[idx, c * chunk:c * chunk + n_rows, g * gw:(g + 1) * gw]

    ax = {cg: expand(ld(0, *cg)).astype(BF16) for cg in pairs}
    rx = {cg: expand(ld(1, *cg)).astype(BF16) for cg in pairs}
    bt = {cg: tile_rows(ld(2, *cg)).astype(BF16) for cg in pairs}
    kt = {cg: tile_rows(ld(3, *cg)).astype(BF16) for cg in pairs}
    vx = {cg: expand(ld(6, *cg)).astype(BF16) for cg in pairs}
    n_mats = [jnp.where(strict, _dot_nt(ax[cg], bt[cg]), 0.0) for cg in pairs]
    a_ak = {cg: jnp.where(strict, _dot_nt(ax[cg], kt[cg]), 0.0).astype(BF16) for cg in pairs}
    a_rb = {cg: jnp.where(lower, _dot_nt(rx[cg], bt[cg]), 0.0).astype(BF16) for cg in pairs}
    a_rk = {cg: jnp.where(lower, _dot_nt(rx[cg], kt[cg]), 0.0).astype(BF16) for cg in pairs}
    t_inv = dict(zip(pairs, (x.astype(BF16)
                             for x in _unit_lower_inverse(n_mats, eye, blk8_r, blk8_c, chunk))))
    akv = {cg: _dot(a_ak[cg], vx[cg]) for cg in pairs}
    rkv = {cg: _dot(a_rk[cg], vx[cg]) for cg in pairs}
    state = [state_ref[g] for g in range(n_groups)]
    for c in range(n_chunks):
        ux = {}
        for g in range(n_groups):
            rhs = expand(_dot_nt(ld(0, c, g), state[g])) + akv[c, g]
            ux[g] = _dot(t_inv[c, g], rhs)
        for g in range(n_groups):
            y = _dot_nt(ld(1, c, g), state[g]) + collapse(_dot(a_rb[c, g], ux[g]) + rkv[c, g])
            y_ref[c * chunk:(c + 1) * chunk, g * gw:(g + 1) * gw] = y
        for g in range(n_groups):
            uv = jnp.concatenate([collapse(ux[g]), ld(6, c, g)], axis=0)
            bk = jnp.concatenate([ld(4, c, g), ld(5, c, g)], axis=0)
            state[g] = state[g] * ld(7, c, g, 1) + jnp.where(state_mask, _dot_tn(uv, bk), 0.0)
    for g in range(n_groups):
        state_ref[g] = state[g]

    y = y_ref[...]
    head_avg = (head_ones.astype(F32) * (1.0 / HEAD_DIM)).astype(BF16)
    mu_y = _dot_sel_rhs(y, head_avg)
    d = y - mu_y
    var_y = _dot_sel_rhs(d * d, head_avg, pieces=2)
    yn = d * lax.rsqrt(var_y + GN_EPS) * gnw_ref[...] + gnb_ref[...]
    bonus = _dot_sel_rhs(r * k2 * rk_ref[...], head_ones, pieces=2)
    o_ref[0] = (yn + bonus * v) * gate

    @pl.when(t == pl.num_programs(1) - 1)
    def _():
        pick = jnp.where(_iota((gw, HEAD_DIM), 0) % HEAD_DIM == _iota((gw, HEAD_DIM), 1),
                         1.0, 0.0).astype(BF16)
        for g in range(n_groups):
            wkv_ref[0, g * gw:(g + 1) * gw, :] = _dot_sel_rhs(state_ref[g], pick)


def _rwkv(p_rwkv, shift0, wkv0, prm):
    bsz, t, _ = p_rwkv.shape
    if t >= RWKV_CHUNK:
        chunk, heads_per_group, tb = RWKV_CHUNK, 4, min(t, ROW_TILE)
    else:
        chunk, heads_per_group, tb = t, N_HEADS, t
    gw = heads_per_group * HEAD_DIM
    n_groups = N_HEADS // heads_per_group
    row = lambda width: pl.BlockSpec((1, tb, width), lambda b, i: (b, i, 0))
    per_b = lambda shape: pl.BlockSpec((1,) + shape, lambda b, i: (b, 0, 0))
    vec = lambda width: _const_spec((1, width))
    o, wkv, shift = pl.pallas_call(
        functools.partial(_rwkv_kernel, chunk=chunk, heads_per_group=heads_per_group, tb=tb),
        grid=(bsz, t // tb),
        in_specs=[row(C_RWKV), per_b((1, C_RWKV)), per_b((WIDTH, HEAD_DIM)),
                  vec(C_RWKV), vec(WIDTH), _const_spec((LANES, 2 * WIDTH)), vec(WIDTH),
                  _const_spec((R_GATE, WIDTH)), vec(WIDTH), vec(WIDTH), vec(WIDTH), vec(WIDTH),
                  vec(WIDTH)],
        out_specs=[row(WIDTH), per_b((WIDTH, HEAD_DIM)), per_b((1, C_RWKV))],
        out_shape=[jax.ShapeDtypeStruct((bsz, t, WIDTH), F32),
                   jax.ShapeDtypeStruct((bsz, WIDTH, HEAD_DIM), F32),
                   jax.ShapeDtypeStruct((bsz, 1, C_RWKV), F32)],
        scratch_shapes=[pltpu.VMEM((n_groups, gw, gw), F32), pltpu.VMEM((1, C_RWKV), F32),
                        pltpu.VMEM((8, tb, WIDTH), F32), pltpu.VMEM((tb, WIDTH), F32)],
        compiler_params=_params("arbitrary", "arbitrary"),
        name="rwkv7",
    )(p_rwkv, shift0.reshape(bsz, 1, C_RWKV), wkv0.reshape(bsz, WIDTH, HEAD_DIM),
      prm["mu"], prm["w0"], prm["lora"], prm["a0"], prm["gup"], prm["kk"], prm["ka"], prm["rk"],
      prm["gnw"], prm["gnb"])
    return (o, wkv.reshape(bsz, N_HEADS, HEAD_DIM, HEAD_DIM), shift.reshape(bsz, C_RWKV))


def _merge_kernel(x_ref, oa_ref, ob_ref, ga_ref, gb_ref, wa_ref, wb_ref, wo_ref, lnw_ref, lnb_ref,
                  y_ref, *, alpha):
    merged = ga_ref[0] * _dot(oa_ref[0], wa_ref[...]) + gb_ref[0] * _dot(ob_ref[0], wb_ref[...])
    y = alpha * x_ref[0] + _dot(merged, wo_ref[...])
    y_ref[0] = _layer_norm(y, lnw_ref[...], lnb_ref[...])


def _merge(x, o_a, o_b, g_a, g_b, w_a, w_b, w_o, ln_w, ln_b, alpha):
    bsz, t, d = x.shape
    tm = min(t, ROW_TILE)
    row = lambda width: pl.BlockSpec((1, tm, width), lambda b, i: (b, i, 0))
    return pl.pallas_call(
        functools.partial(_merge_kernel, alpha=alpha),
        grid=(bsz, t // tm),
        in_specs=[row(d), row(WIDTH), row(WIDTH), row(d), row(d),
                  _const_spec(w_a.shape), _const_spec(w_b.shape), _const_spec(w_o.shape),
                  _const_spec((1, d)), _const_spec((1, d))],
        out_specs=row(d),
        out_shape=jax.ShapeDtypeStruct((bsz, t, d), F32),
        compiler_params=_params("arbitrary", "arbitrary"),
        name="merge_ln",
    )(x, o_a, o_b, g_a, g_b, w_a, w_b, w_o, ln_w, ln_b)


def _gelu_tanh(x):
    return x * (0.5 * (1.0 + jnp.tanh(math.sqrt(2.0 / math.pi) * (x + 0.044715 * (x * x * x)))))


def _ffn_kernel(x_ref, conv0_ref, wup_ref, cw_ref, cb_ref, wdn_ref, lnw_ref, lnb_ref,
                y_ref, conv_ref, tail_ref, *, alpha, tm):
    t = pl.program_id(1)
    d_ff = cb_ref.shape[-1]
    n_tail = CONV_W - 1

    @pl.when(t == 0)
    def _():
        tail_ref[...] = conv0_ref[0]

    x = x_ref[0]
    xb = x.astype(BF16)
    u = jnp.dot(xb, wup_ref[:, :d_ff], preferred_element_type=F32)
    gate = jnp.dot(xb, wup_ref[:, d_ff:], preferred_element_type=F32)
    rowi = _iota((tm, d_ff), 0)
    conv = cb_ref[...] + cw_ref[n_tail:n_tail + 1, :] * u
    for back in range(1, CONV_W):
        prev = pltpu.roll(u, back, 0)
        for j in range(back):
            prev = jnp.where(rowi == j, tail_ref[n_tail - back + j:n_tail - back + j + 1, :], prev)
        conv = conv + cw_ref[n_tail - back:n_tail - back + 1, :] * prev
    tail_ref[...] = u[tm - n_tail:tm, :]
    conv_ref[0] = u[tm - n_tail:tm, :]
    hmid = _gelu_tanh(conv) * gate
    y = alpha * x + _dot(hmid, wdn_ref[...])
    y_ref[0] = _layer_norm(y, lnw_ref[...], lnb_ref[...])


def _ffn(x, conv0, w_up, conv_w, conv_b, w_down, ln_w, ln_b, alpha):
    bsz, t, d = x.shape
    d_ff = w_down.shape[0]
    tm = min(t, ROW_TILE)
    assert tm >= CONV_W - 1
    row = lambda width: pl.BlockSpec((1, tm, width), lambda b, i: (b, i, 0))
    per_b = lambda shape: pl.BlockSpec((1,) + shape, lambda b, i: (b, 0, 0))
    return pl.pallas_call(
        functools.partial(_ffn_kernel, alpha=alpha, tm=tm),
        grid=(bsz, t // tm),
        in_specs=[row(d), per_b((CONV_W - 1, d_ff)), _const_spec(w_up.shape),
                  _const_spec((CONV_W, d_ff)), _const_spec((1, d_ff)), _const_spec(w_down.shape),
                  _const_spec((1, d)), _const_spec((1, d))],
        out_specs=[row(d), per_b((CONV_W - 1, d_ff))],
        out_shape=[jax.ShapeDtypeStruct((bsz, t, d), F32),
                   jax.ShapeDtypeStruct((bsz, CONV_W - 1, d_ff), F32)],
        scratch_shapes=[pltpu.VMEM((CONV_W - 1, d_ff), F32)],
        compiler_params=_params("arbitrary", "arbitrary"),
        name="conv_ffn_ln",
    )(x, conv0, w_up, conv_w, conv_b, w_down, ln_w, ln_b)


def _layer_params(l, w_in, b_forget, mu_shift, decay_w0, decay_up, iclr_a0, iclr_up, gate_up,
                  k_k, k_a, r_k, gn_w, gn_b, w_branch_a, w_branch_b, w_out, ln1_w, ln1_b,
                  w_up, conv_w, conv_b, w_down, ln2_w, ln2_b):
    d = w_in.shape[1]
    c_fox = 3 * WIDTH + N_HEADS
    w = w_in[l]
    w_pad = jnp.concatenate([w[:, :c_fox], jnp.zeros((d, LANES - N_HEADS), F32), w[:, c_fox:]], axis=1)
    bf = jnp.zeros((1, LANES), F32).at[0, :N_HEADS].set(b_forget[l])
    zeros = jnp.zeros((R_DECAY, WIDTH), F32)
    lora = jnp.concatenate([jnp.concatenate([decay_up[l], zeros], axis=1),
                            jnp.concatenate([zeros, iclr_up[l]], axis=1)], axis=0)
    vec = lambda z: z.reshape(1, -1)
    return dict(
        w_in=w_pad.astype(BF16), bf=bf,
        rwkv=dict(mu=vec(mu_shift[l]), w0=vec(decay_w0[l]), lora=lora.astype(BF16), a0=vec(iclr_a0[l]),
                  gup=gate_up[l].astype(BF16), kk=vec(k_k[l]), ka=vec(k_a[l]), rk=vec(r_k[l]),
                  gnw=vec(gn_w[l]), gnb=vec(gn_b[l])),
        w_a=w_branch_a[l].astype(BF16), w_b=w_branch_b[l].astype(BF16), w_o=w_out[l].astype(BF16),
        ln1_w=vec(ln1_w[l]), ln1_b=vec(ln1_b[l]),
        w_up=w_up[l].astype(BF16), conv_w=conv_w[l], conv_b=vec(conv_b[l]), w_down=w_down[l].astype(BF16),
        ln2_w=vec(ln2_w[l]), ln2_b=vec(ln2_b[l]))


def _run_group(x, layers, attend, shift0, wkv0, conv0, alpha, augment):
    bsz, t, _ = x.shape
    ks, vs, lfs, wkvs, shifts, convs = [], [], [], [], [], []
    for l, prm in enumerate(layers):
        outs = _in_proj(x, prm["w_in"], prm["bf"], augment=augment)
        if augment:
            q_aug, k, v, k_aug, vt_bf, logf, c, p_rwkv, g_a, g_b = outs
            o_a = attend(l, q_aug, k_aug, vt_bf, c)
        else:
            q_bf, k, v, logf, c, p_rwkv, g_a, g_b = outs
            o_a = attend(l, q_bf, k, v, c)
        o_b, wkv_l, shift_l = _rwkv(p_rwkv, shift0[l], wkv0[l], prm["rwkv"])
        x = _merge(x, o_a, o_b, g_a, g_b, prm["w_a"], prm["w_b"], prm["w_o"],
                   prm["ln1_w"], prm["ln1_b"], alpha)
        x, conv_l = _ffn(x, conv0[l], prm["w_up"], prm["conv_w"], prm["conv_b"], prm["w_down"],
                         prm["ln2_w"], prm["ln2_b"], alpha)
        ks.append(k.reshape(bsz, t, N_HEADS, HEAD_DIM))
        vs.append(v.reshape(bsz, t, N_HEADS, HEAD_DIM))
        lfs.append(logf)
        wkvs.append(wkv_l)
        shifts.append(shift_l)
        convs.append(conv_l)
    return (x, jnp.stack(ks), jnp.stack(vs), jnp.stack(lfs), jnp.stack(wkvs),
            jnp.stack(shifts), jnp.stack(convs))


def kernel(x_prompt, x_sample, cache_k, cache_v, cache_logf, state_wkv, state_shift, state_conv, page_table, w_in, b_forget, mu_shift, decay_w0, decay_up, iclr_a0, iclr_up, gate_up, k_k, k_a, r_k, gn_w, gn_b, w_branch_a, w_branch_b, w_out, ln1_w, ln1_b, w_up, conv_w, conv_b, w_down, ln2_w, ln2_b):
    depth = w_in.shape[0]
    alpha = (2.0 * depth) ** 0.25
    layers = [_layer_params(l, w_in, b_forget, mu_shift, decay_w0, decay_up, iclr_a0, iclr_up,
                            gate_up, k_k, k_a, r_k, gn_w, gn_b, w_branch_a, w_branch_b, w_out,
                            ln1_w, ln1_b, w_up, conv_w, conv_b, w_down, ln2_w, ln2_b)
              for l in range(depth)]

    n_pool = cache_k.shape[1]
    flat_k = cache_k.transpose(0, 1, 3, 4, 2).reshape(depth * n_pool, WIDTH, PAGE_SIZE)
    flat_v = cache_v.transpose(0, 1, 3, 4, 2).reshape(depth * n_pool, WIDTH, PAGE_SIZE)
    flat_lf = cache_logf.transpose(0, 1, 3, 2).reshape(depth * n_pool, N_HEADS, PAGE_SIZE)

    def attend_prompt(l, q_aug, k_aug, vt_bf, c):
        return _fox_prompt(q_aug, k_aug, vt_bf)

    def attend_sample(l, q_bf, k, v, c):
        return _fox_cached(q_bf, k, v, c, flat_k, flat_v, flat_lf, page_table + l * n_pool)

    bp, _, d = x_prompt.shape
    d_ff = w_down.shape[1]
    prompt = _run_group(x_prompt, layers, attend_prompt,
                        jnp.zeros((depth, bp, C_RWKV), F32),
                        jnp.zeros((depth, bp, N_HEADS, HEAD_DIM, HEAD_DIM), F32),
                        jnp.zeros((depth, bp, CONV_W - 1, d_ff), F32), alpha, True)
    sample = _run_group(x_sample, layers, attend_sample, state_shift, state_wkv, state_conv,
                        alpha, False)
    (y_p, k_p, v_p, lf_p, wkv_p, shift_p, conv_p) = prompt
    (y_s, k_s, v_s, lf_s, wkv_s, shift_s, conv_s) = sample
    return (y_p, y_s, k_p, v_p, lf_p, wkv_p, shift_p, conv_p,
            k_s, v_s, lf_s, wkv_s, shift_s, conv_s)
```

```python
import functools
import math

import jax
import jax.numpy as jnp
import numpy as np
from jax import lax
from jax.experimental import pallas as pl
from jax.experimental.pallas import tpu as pltpu

F32 = jnp.float32
BF16 = jnp.bfloat16

N_HEADS = 8
HEAD_DIM = 64
WIDTH = N_HEADS * HEAD_DIM
R_DECAY = 64
R_ICLR = 64
R_GATE = 128
C_RWKV = 3 * WIDTH + R_DECAY + R_ICLR + R_GATE
CONV_W = 3
PAGE_SIZE = 128
LN_EPS = 1e-5
GN_EPS = 64e-5
NEG_BIG = -1e30

LOG2E = math.log2(math.e)

LANES = 128
ROW_TILE = 256
ATTN_TILE = 512
ATTN_SUB = 256
V_ROWS = HEAD_DIM + 16
RWKV_CHUNK = 64
PAGES_PER_STEP = 16
VMEM_LIMIT_BYTES = 56 * 1024 * 1024

_OFF_Q, _OFF_K, _OFF_V, _OFF_F = 0, WIDTH, 2 * WIDTH, 3 * WIDTH
_OFF_RWKV = 3 * WIDTH + LANES
_OFF_GA = _OFF_RWKV + C_RWKV


def _iota(shape, dim):
    return lax.broadcasted_iota(jnp.int32, shape, dim)


def _dot(a, b):
    return jnp.dot(a.astype(BF16), b.astype(BF16), preferred_element_type=F32)


def _dot_nt(a, b):
    return lax.dot_general(a.astype(BF16), b.astype(BF16), (((1,), (1,)), ((), ())),
                           preferred_element_type=F32)


def _dot_tn(a, b):
    return lax.dot_general(a.astype(BF16), b.astype(BF16), (((0,), (0,)), ((), ())),
                           preferred_element_type=F32)


def _split3(x):
    hi = x.astype(BF16)
    r1 = x - hi.astype(F32)
    mid = r1.astype(BF16)
    lo = (r1 - mid.astype(F32)).astype(BF16)
    return hi, mid, lo


def _dot_sel_rhs(x, sel, pieces=3):
    hi, mid, lo = _split3(x)
    d = lambda p: jnp.dot(p, sel, preferred_element_type=F32)
    if pieces == 2:
        return d(hi) + d(mid)
    return d(hi) + (d(mid) + d(lo))


def _dot_sel_lhs(sel, x):
    hi, mid, lo = _split3(x)
    d = lambda p: jnp.dot(sel, p, preferred_element_type=F32)
    return d(hi) + (d(mid) + d(lo))


def _sigmoid(x):
    return 1.0 / (1.0 + jnp.exp(-x))


def _log_sigmoid(x):
    return jnp.minimum(x, 0.0) - jnp.log1p(jnp.exp(-jnp.abs(x)))


def _layer_norm(x, w, b):
    mu = jnp.mean(x, axis=-1, keepdims=True)
    d = x - mu
    var = jnp.mean(d * d, axis=-1, keepdims=True)
    return d * lax.rsqrt(var + LN_EPS) * w + b


def _params(*sem):
    return pltpu.CompilerParams(dimension_semantics=sem, vmem_limit_bytes=VMEM_LIMIT_BYTES)


def _const_spec(shape):
    zeros = (0,) * len(shape)
    return pl.BlockSpec(shape, lambda *_: zeros, pipeline_mode=pl.Buffered(1))


def _aug_constants():
    place_q = np.zeros((3 * LANES, N_HEADS * LANES), np.float32)
    place_k = np.zeros((3 * LANES, N_HEADS * LANES), np.float32)
    ones_q = np.zeros((1, N_HEADS * LANES), np.float32)
    ones_k = np.zeros((1, N_HEADS * LANES), np.float32)
    for h in range(N_HEADS):
        base = LANES * h + (HEAD_DIM if h % 2 == 0 else 0)
        for piece in range(3):
            place_q[piece * LANES + h, base + piece] = 1.0
            place_k[piece * LANES + h, base + 3 + piece] = -1.0
            ones_q[0, base + 3 + piece] = 1.0
            ones_k[0, base + piece] = 1.0
    return (jnp.asarray(place_q, BF16), jnp.asarray(place_k, BF16),
            jnp.asarray(ones_q), jnp.asarray(ones_k))


def _in_proj_kernel(x_ref, w_ref, bf_ref, *refs, tm, augment):
    if augment:
        (pq_ref, pk_ref, oq_ref, ok_ref,
         q_ref, k_ref, v_ref, kaug_ref, vt_ref, logf_ref, c_ref, prw_ref, ga_ref, gb_ref, carry_ref) = refs
    else:
        (q_ref, k_ref, v_ref, logf_ref, c_ref, prw_ref, ga_ref, gb_ref, carry_ref) = refs
    t = pl.program_id(1)
    x = x_ref[0].astype(BF16)

    def proj(lo, hi):
        return jnp.dot(x, w_ref[:, lo:hi], preferred_element_type=F32)

    q = proj(_OFF_Q, _OFF_Q + WIDTH)
    k = proj(_OFF_K, _OFF_K + WIDTH)
    k_ref[0] = k
    v = proj(_OFF_V, _OFF_V + WIDTH)
    v_ref[0] = v
    if augment:
        vt = v.T.astype(BF16)
        extra = jnp.where(_iota((V_ROWS - HEAD_DIM, tm), 0) == 0, 1.0, 0.0).astype(BF16)
        vt_ref[0, 0] = jnp.concatenate(
            [z for h in range(N_HEADS) for z in (vt[HEAD_DIM * h:HEAD_DIM * (h + 1)], extra)], axis=0)
    else:
        q_ref[0] = q * HEAD_DIM ** -0.5

    z = proj(_OFF_F, _OFF_F + LANES) + bf_ref[...]
    lane = _iota((tm, LANES), 1)
    lf = jnp.where(lane < N_HEADS, _log_sigmoid(z), 0.0)
    logf_ref[0] = lf[:, :N_HEADS]

    @pl.when(t == 0)
    def _():
        carry_ref[...] = jnp.zeros_like(carry_ref)

    tril = jnp.where(_iota((tm, tm), 1) <= _iota((tm, tm), 0), 1.0, 0.0).astype(BF16)
    cs = _dot_sel_lhs(tril, lf) + carry_ref[...]
    carry_ref[...] = cs[tm - 1:tm, :]
    c_ref[0] = cs[:, :N_HEADS]

    if augment:
        pieces = jnp.concatenate(_split3(cs * LOG2E), axis=1)
        head_lane = _iota((tm, N_HEADS * LANES), 1)
        own = (head_lane % LANES) // HEAD_DIM == (head_lane // LANES) % 2
        pair = lambda z: jnp.concatenate(
            [z[:, LANES * (h // 2):LANES * (h // 2 + 1)] for h in range(N_HEADS)], axis=1)
        aug_q = jnp.dot(pieces, pq_ref[...], preferred_element_type=F32) + oq_ref[...]
        aug_k = jnp.dot(pieces, pk_ref[...], preferred_element_type=F32) + ok_ref[...]
        q_ref[0] = jnp.where(own, pair(q * (HEAD_DIM ** -0.5 * LOG2E)), aug_q).astype(BF16)
        kaug_ref[0] = jnp.where(own, pair(k), aug_k).astype(BF16)

    prw_ref[0] = proj(_OFF_RWKV, _OFF_RWKV + C_RWKV)
    d_model = ga_ref.shape[-1]
    ga_ref[0] = _sigmoid(proj(_OFF_GA, _OFF_GA + d_model))
    gb_ref[0] = _sigmoid(proj(_OFF_GA + d_model, _OFF_GA + 2 * d_model))


def _in_proj(x, w, bf, *, augment):
    bsz, t, d = x.shape
    tm = min(t, ROW_TILE)
    nt = t // tm
    c_all = w.shape[1]
    row = lambda width: pl.BlockSpec((1, tm, width), lambda b, i: (b, i, 0))
    in_specs = [row(d), _const_spec((d, c_all)), _const_spec((1, LANES))]
    operands = [x, w, bf]
    wide = N_HEADS * LANES
    if augment:
        per_tile = ATTN_TILE // tm
        consts = _aug_constants()
        in_specs += [_const_spec(c.shape) for c in consts]
        operands += list(consts)
        out_shape = [jax.ShapeDtypeStruct((bsz, t, wide), BF16),
                     jax.ShapeDtypeStruct((bsz, t, WIDTH), F32),
                     jax.ShapeDtypeStruct((bsz, t, WIDTH), F32),
                     jax.ShapeDtypeStruct((bsz, t, wide), BF16),
                     jax.ShapeDtypeStruct((bsz, t // ATTN_TILE, N_HEADS * V_ROWS, ATTN_TILE), BF16)]
        out_specs = [row(wide), row(WIDTH), row(WIDTH), row(wide),
                     pl.BlockSpec((1, 1, N_HEADS * V_ROWS, tm),
                                  lambda b, i: (b, i // per_tile, 0, i % per_tile))]
    else:
        out_shape = [jax.ShapeDtypeStruct((bsz, t, WIDTH), F32),
                     jax.ShapeDtypeStruct((bsz, t, WIDTH), F32),
                     jax.ShapeDtypeStruct((bsz, t, WIDTH), F32)]
        out_specs = [row(WIDTH), row(WIDTH), row(WIDTH)]
    out_shape += [jax.ShapeDtypeStruct((bsz, t, N_HEADS), F32),
                  jax.ShapeDtypeStruct((bsz, t, N_HEADS), F32),
                  jax.ShapeDtypeStruct((bsz, t, C_RWKV), F32),
                  jax.ShapeDtypeStruct((bsz, t, d), F32),
                  jax.ShapeDtypeStruct((bsz, t, d), F32)]
    out_specs += [row(N_HEADS), row(N_HEADS), row(C_RWKV), row(d), row(d)]
    return pl.pallas_call(
        functools.partial(_in_proj_kernel, tm=tm, augment=augment),
        grid=(bsz, nt),
        in_specs=in_specs,
        out_specs=out_specs,
        out_shape=out_shape,
        scratch_shapes=[pltpu.VMEM((1, LANES), F32)],
        compiler_params=_params("arbitrary", "arbitrary"),
        name="in_proj",
    )(*operands)


def _fox_prompt_kernel(q_ref, k_ref, vt_ref, o_ref, s_ref, m_ref, acc_ref, *, tile):
    i = pl.program_id(2)
    heads = (0, 1)
    key_pos = _iota((ATTN_SUB, tile), 0)
    query_pos = _iota((ATTN_SUB, tile), 1)

    pairs = [(sub, h) for sub in range(tile // ATTN_SUB) for h in heads]

    def score(j, slot):
        start = pl.multiple_of(j * tile, tile)
        for n, (sub, h) in enumerate(pairs):
            lanes = slice(LANES * h, LANES * (h + 1))
            s_ref[slot, n] = _dot_nt(k_ref[0, pl.ds(start + sub * ATTN_SUB, ATTN_SUB), lanes],
                                     q_ref[0, :, lanes])

    def consume(j, slot, masked):
        carry = [(m_ref[h], acc_ref[h]) for h in heads]
        probs = []
        scale = []
        for n, (sub, h) in enumerate(pairs):
            s = s_ref[slot, n]
            if masked:
                s = jnp.where(key_pos + sub * ATTN_SUB <= query_pos, s, NEG_BIG)
            m, acc = carry[h]
            m_new = jnp.maximum(m, jnp.max(s, axis=0, keepdims=True))
            probs.append(jnp.exp2(s - m_new).astype(BF16))
            scale.append(jnp.exp2(m - m_new))
            carry[h] = (m_new, acc)
        for n, (sub, h) in enumerate(pairs):
            m_new, acc = carry[h]
            vt = vt_ref[0, j, V_ROWS * h:V_ROWS * (h + 1), sub * ATTN_SUB:(sub + 1) * ATTN_SUB]
            carry[h] = (m_new, scale[n] * acc + jnp.dot(vt, probs[n], preferred_element_type=F32))
        for h in heads:
            m_ref[h], acc_ref[h] = carry[h]

    for h in heads:
        m_ref[h] = jnp.full((1, tile), NEG_BIG, F32)
        acc_ref[h] = jnp.zeros((V_ROWS, tile), F32)

    score(0, 0)

    def two_tiles(jj, _):
        j = 2 * jj
        score(j + 1, 1)
        consume(j, 0, False)
        score(j + 2, 0)
        consume(j + 1, 1, False)
        return 0

    lax.fori_loop(0, i // 2, two_tiles, 0)
    odd = i % 2 == 1

    @pl.when(odd)
    def _():
        score(i, 1)
        consume(i - 1, 0, False)
        consume(i, 1, True)

    @pl.when(jnp.logical_not(odd))
    def _():
        consume(i, 0, True)

    o_ref[0] = jnp.concatenate(
        [acc_ref[h, :HEAD_DIM] / acc_ref[h, HEAD_DIM:HEAD_DIM + 1] for h in heads], axis=0).T


def _fox_prompt(q_aug, k_aug, vt_bf):
    bsz, t, _ = q_aug.shape
    tile = vt_bf.shape[-1]
    nq = t // tile
    n_pairs = N_HEADS // 2
    return pl.pallas_call(
        functools.partial(_fox_prompt_kernel, tile=tile),
        grid=(bsz, n_pairs, nq),
        in_specs=[pl.BlockSpec((1, tile, 2 * LANES), lambda b, hp, i: (b, i, hp)),
                  pl.BlockSpec((1, t, 2 * LANES), lambda b, hp, i: (b, 0, hp)),
                  pl.BlockSpec((1, nq, 2 * V_ROWS, tile), lambda b, hp, i: (b, 0, hp, 0))],
        out_specs=pl.BlockSpec((1, tile, LANES), lambda b, hp, i: (b, i, hp)),
        out_shape=jax.ShapeDtypeStruct((bsz, t, WIDTH), F32),
        scratch_shapes=[pltpu.VMEM((2, 2 * (tile // ATTN_SUB), ATTN_SUB, tile), F32),
                        pltpu.VMEM((2, 1, tile), F32), pltpu.VMEM((2, V_ROWS, tile), F32)],
        compiler_params=_params("arbitrary", "arbitrary", "arbitrary"),
        name="fox_prompt",
    )(q_aug, k_aug, vt_bf)


def _fox_cached_kernel(pt_ref, q_ref, knew_ref, vnew_ref, cq_ref, ck_ref, later_ref, *refs, n_group):
    del pt_ref
    kp = refs[:n_group]
    vp = refs[n_group:2 * n_group]
    lp = refs[2 * n_group:3 * n_group]
    o_ref, m_ref, l_ref, acc_ref, carry_ref, both_ref = refs[3 * n_group:]
    step = pl.program_id(1)
    t_new = q_ref.shape[1]
    rows = N_HEADS * t_new
    own_head = _iota((rows, WIDTH), 0) // t_new == _iota((rows, WIDTH), 1) // HEAD_DIM
    qx = jnp.where(own_head, jnp.concatenate([q_ref[0]] * N_HEADS, axis=0), 0.0).astype(BF16)

    @pl.when(step == 0)
    def _():
        s = _dot_nt(qx, knew_ref[0])
        s = s + (cq_ref[0] - ck_ref[0])
        qpos = _iota((rows, t_new), 0) % t_new
        s = jnp.where(_iota((rows, t_new), 1) <= qpos, s, NEG_BIG)
        m = jnp.max(s, axis=1, keepdims=True)
        p = jnp.exp(s - m)
        m_ref[...] = m
        l_ref[...] = jnp.sum(p, axis=1, keepdims=True)
        acc_ref[...] = _dot(p, vnew_ref[0])
        carry_ref[...] = jnp.broadcast_to(cq_ref[0], carry_ref.shape)

    groups = range(n_group)
    raw = [_dot(qx, kp[g][0]) for g in groups]
    lf_t = jnp.concatenate([lp[g][0] for g in groups], axis=0)
    both_ref[...] = _dot_sel_rhs(lf_t, later_ref[...])
    carry = carry_ref[...]
    scores = []
    for g in groups:
        both = jnp.concatenate(
            [jnp.broadcast_to(both_ref[g * N_HEADS + h:g * N_HEADS + h + 1, :], (t_new, 2 * PAGE_SIZE))
             for h in range(N_HEADS)], axis=0)
        scores.append(raw[g] + (both[:, :PAGE_SIZE] + carry))
        carry = carry + both[:, PAGE_SIZE:]
    carry_ref[...] = carry

    smax = scores[0]
    for s in scores[1:]:
        smax = jnp.maximum(smax, s)
    m_old = m_ref[...]
    m_new = jnp.maximum(m_old, jnp.max(smax, axis=1, keepdims=True))
    alpha = jnp.exp(m_old - m_new)
    l = alpha * l_ref[...]
    acc = alpha * acc_ref[...]
    probs = [jnp.exp(scores[g] - m_new) for g in groups]
    for g in groups:
        l = l + jnp.sum(probs[g], axis=1, keepdims=True)
        acc = acc + _dot_nt(probs[g], vp[g][0])
    m_ref[...] = m_new
    l_ref[...] = l
    acc_ref[...] = acc

    @pl.when(step == pl.num_programs(1) - 1)
    def _():
        o = jnp.where(own_head, acc / l, 0.0)
        out = o[0:t_new]
        for h in range(1, N_HEADS):
            out = out + o[h * t_new:(h + 1) * t_new]
        o_ref[0] = out


def _fox_cached(q_bf, k_new, v_new, c_new, cache_k, cache_v, cache_logf, pages):
    bsz, t_new, _ = q_bf.shape
    n_pages = pages.shape[1]
    n_group = math.gcd(PAGES_PER_STEP, n_pages)
    n_steps = n_pages // n_group
    rows = N_HEADS * t_new
    c_q = c_new.transpose(0, 2, 1).reshape(bsz, rows, 1)
    c_k = jnp.repeat(c_new.transpose(0, 2, 1), t_new, axis=1)

    def page_map(g, n_trailing):
        zeros = (0,) * n_trailing
        return lambda b, s, pt: (pt[b, n_pages - 1 - (s * n_group + g)],) + zeros

    def per_b(shape):
        zeros = (0,) * len(shape)
        return pl.BlockSpec((1,) + shape, lambda b, s, pt: (b,) + zeros)

    keys = np.arange(PAGE_SIZE)
    later = np.concatenate([(keys[:, None] > keys[None, :]).astype(np.float32),
                            np.ones((PAGE_SIZE, PAGE_SIZE), np.float32)], axis=1)
    later = jnp.asarray(later, BF16)

    tok_blk = (t_new, WIDTH)
    in_specs = [per_b(tok_blk), per_b(tok_blk), per_b(tok_blk), per_b((rows, 1)), per_b((rows, t_new)),
                pl.BlockSpec(later.shape, lambda b, s, pt: (0, 0), pipeline_mode=pl.Buffered(1))]
    kv_blk = (1, WIDTH, PAGE_SIZE)
    in_specs += [pl.BlockSpec(kv_blk, page_map(g, 2)) for g in range(n_group)]
    in_specs += [pl.BlockSpec(kv_blk, page_map(g, 2)) for g in range(n_group)]
    in_specs += [pl.BlockSpec((1, N_HEADS, PAGE_SIZE), page_map(g, 2)) for g in range(n_group)]
    return pl.pallas_call(
        functools.partial(_fox_cached_kernel, n_group=n_group),
        grid_spec=pltpu.PrefetchScalarGridSpec(
            num_scalar_prefetch=1,
            grid=(bsz, n_steps),
            in_specs=in_specs,
            out_specs=per_b(tok_blk),
            scratch_shapes=[pltpu.VMEM((rows, 1), F32), pltpu.VMEM((rows, 1), F32),
                            pltpu.VMEM((rows, WIDTH), F32), pltpu.VMEM((rows, PAGE_SIZE), F32),
                            pltpu.VMEM((n_group * N_HEADS, later.shape[1]), F32)]),
        out_shape=jax.ShapeDtypeStruct((bsz,) + tok_blk, F32),
        compiler_params=_params("arbitrary", "arbitrary"),
        name="fox_cached",
    )(pages, q_bf, k_new, v_new, c_q, c_k, later,
      *([cache_k] * n_group), *([cache_v] * n_group), *([cache_logf] * n_group))


def _unit_lower_inverse(n_mats, eye, blk_of_row, blk_of_col, chunk):
    base = 8
    same = lambda size: (blk_of_row // (size // base)) == (blk_of_col // (size // base))
    ms = [jnp.where(same(base), n, 0.0) for n in n_mats]
    xs = [eye + m for m in ms]
    for _ in range(2):
        ms = [_dot(m, m) for m in ms]
        xs = [x + _dot(x, m) for x, m in zip(xs, ms)]
    size = base
    while size < chunk:
        ring = same(2 * size) & jnp.logical_not(same(size))
        half = [_dot(x, jnp.where(ring, n, 0.0)) for x, n in zip(xs, n_mats)]
        xs = [x + _dot(h, x) for x, h in zip(xs, half)]
        size *= 2
    return xs


def _rwkv_kernel(p_ref, shift0_ref, wkv0_ref, mu_ref, w0_ref, lora_ref, a0_ref, gup_ref,
                 kk_ref, ka_ref, rk_ref, gnw_ref, gnb_ref,
                 o_ref, wkv_ref, shift_ref,
                 state_ref, prev_ref, pre_ref, y_ref, *, chunk, heads_per_group, tb):
    t = pl.program_id(1)
    gw = heads_per_group * HEAD_DIM
    n_groups = N_HEADS // heads_per_group
    gl = heads_per_group * chunk
    n_chunks = tb // chunk

    row_head = _iota((gw, gw), 0) // HEAD_DIM
    col_head = _iota((gw, gw), 1) // HEAD_DIM
    state_mask = row_head == col_head

    @pl.when(t == 0)
    def _():
        prev_ref[...] = shift0_ref[0]
        tile_lanes = jnp.where(_iota((HEAD_DIM, gw), 0) == _iota((HEAD_DIM, gw), 1) % HEAD_DIM,
                               1.0, 0.0).astype(BF16)
        for g in range(n_groups):
            s0 = wkv0_ref[0, g * gw:(g + 1) * gw, :]
            state_ref[g] = jnp.where(state_mask, _dot_sel_rhs(s0, tile_lanes), 0.0)

    p = p_ref[0]
    first_row = _iota((tb, C_RWKV), 0) == 0
    shifted = jnp.where(first_row, prev_ref[...], pltpu.roll(p, 1, 0))
    prev_ref[...] = p[tb - 1:tb, :]
    shift_ref[0] = p[tb - 1:tb, :]
    seg_r = _iota((WIDTH, WIDTH), 0) // HEAD_DIM
    seg_c = _iota((WIDTH, WIDTH), 1) // HEAD_DIM
    head_ones = jnp.where(seg_r == seg_c, 1.0, 0.0).astype(BF16)

    def pre():
        rows = slice(0, tb)
        xs = p + mu_ref[...] * (shifted - p)
        r = xs[:, 0:WIDTH]
        k = xs[:, WIDTH:2 * WIDTH]
        v = xs[:, 2 * WIDTH:3 * WIDTH]
        wa = xs[:, 3 * WIDTH:3 * WIDTH + LANES]
        gd = xs[:, 3 * WIDTH + LANES:]
        lane = _iota((tb, LANES), 1)
        lora = _dot(jnp.where(lane < R_DECAY, jnp.tanh(wa), wa), lora_ref[...])
        w_raw = _log_sigmoid(w0_ref[...] + lora[:, :WIDTH]) - 0.5
        lw = -jnp.exp(w_raw)
        iclr = _sigmoid(a0_ref[...] + lora[:, WIDTH:])
        kk = k * kk_ref[...]
        kk_norm = jnp.sqrt(_dot_sel_rhs(kk * kk, head_ones, pieces=2))
        kk = kk / jnp.maximum(kk_norm, 1e-12)
        k2 = k * (1.0 + (iclr - 1.0) * ka_ref[...])
        a = -kk
        b = kk * iclr
        sq = (tb, tb)
        incl = jnp.where((_iota(sq, 0) // chunk == _iota(sq, 1) // chunk) & (_iota(sq, 1) <= _iota(sq, 0)),
                         1.0, 0.0).astype(BF16)
        cum = _dot_sel_lhs(incl, lw)
        cum_end = jnp.concatenate(
            [jnp.broadcast_to(cum[(c + 1) * chunk - 1:(c + 1) * chunk, :], (chunk, WIDTH))
             for c in range(n_chunks)], axis=0)
        pre_ref[0, rows] = a * jnp.exp(cum - lw)
        pre_ref[1, rows] = r * jnp.exp(cum)
        pre_ref[2, rows] = b * jnp.exp(-cum)
        pre_ref[3, rows] = k2 * jnp.exp(-cum)
        pre_ref[4, rows] = b * jnp.exp(cum_end - cum)
        pre_ref[5, rows] = k2 * jnp.exp(cum_end - cum)
        pre_ref[6, rows] = v
        pre_ref[7, rows] = jnp.exp(cum_end)
        pre_ref[8, rows] = r * k2 * rk_ref[...]
        pre_ref[9, rows] = _dot(_sigmoid(gd), gup_ref[...])

    e_row_head = _iota((gl, gw), 0) // chunk
    e_col_head = _iota((gl, gw), 1) // HEAD_DIM
    expand_mask = e_row_head == e_col_head
    sq_r = _iota((gl, gl), 0)
    sq_c = _iota((gl, gl), 1)
    same_head = (sq_r // chunk) == (sq_c // chunk)
    strict = same_head & (sq_c < sq_r)
    lower = same_head & (sq_c <= sq_r)
    eye = jnp.where(sq_r == sq_c, 1.0, 0.0)
    blk8_r = sq_r // 8
    blk8_c = sq_c // 8

    def expand(x):
        return jnp.where(expand_mask, jnp.concatenate([x] * heads_per_group, axis=0), 0.0)

    def collapse(x):
        out = x[0:chunk]
        for h in range(1, heads_per_group):
            out = out + x[h * chunk:(h + 1) * chunk]
        return out

    def tile_rows(x):
        return jnp.concatenate([x] * heads_per_group, axis=0)

    def ld(idx, c, g, n_rows=chunk):
        return pre_ref[idx, c * chunk:c * chunk + n_rows, g * gw:(g + 1) * gw]

    def independent():
        pairs = [(c, g) for c in range(n_chunks) for g in range(n_groups)]
        ax = {cg: expand(ld(0, *cg)).astype(BF16) for cg in pairs}
        rx = {cg: expand(ld(1, *cg)).astype(BF16) for cg in pairs}
        bt = {cg: tile_rows(ld(2, *cg)).astype(BF16) for cg in pairs}
        kt = {cg: tile_rows(ld(3, *cg)).astype(BF16) for cg in pairs}
        vx = {cg: expand(ld(6, *cg)).astype(BF16) for cg in pairs}
        n_mats = [jnp.where(strict, _dot_nt(ax[cg], bt[cg]), 0.0) for cg in pairs]
        a_ak = {cg: jnp.where(strict, _dot_nt(ax[cg], kt[cg]), 0.0).astype(BF16) for cg in pairs}
        a_rb = {cg: jnp.where(lower, _dot_nt(rx[cg], bt[cg]), 0.0).astype(BF16) for cg in pairs}
        a_rk = {cg: jnp.where(lower, _dot_nt(rx[cg], kt[cg]), 0.0).astype(BF16) for cg in pairs}
        t_inv = dict(zip(pairs, (x.astype(BF16)
                                 for x in _unit_lower_inverse(n_mats, eye, blk8_r, blk8_c, chunk))))
        akv = {cg: _dot(a_ak[cg], vx[cg]) for cg in pairs}
        rkv = {cg: _dot(a_rk[cg], vx[cg]) for cg in pairs}
        return t_inv, a_rb, akv, rkv

    def chain(mats, state):
        t_inv, a_rb, akv, rkv = mats
        for c in range(n_chunks):
            ux = {}
            for g in range(n_groups):
                rhs = expand(_dot_nt(ld(0, c, g), state[g])) + akv[c, g]
                ux[g] = _dot(t_inv[c, g], rhs)
            for g in range(n_groups):
                y = _dot_nt(ld(1, c, g), state[g]) + collapse(_dot(a_rb[c, g], ux[g]) + rkv[c, g])
                y_ref[c * chunk:(c + 1) * chunk, g * gw:(g + 1) * gw] = y
            for g in range(n_groups):
                uv = jnp.concatenate([collapse(ux[g]), ld(6, c, g)], axis=0)
                bk = jnp.concatenate([ld(4, c, g), ld(5, c, g)], axis=0)
                state[g] = state[g] * ld(7, c, g, 1) + jnp.where(state_mask, _dot_tn(uv, bk), 0.0)
        return state

    def post():
        rows = slice(0, tb)
        y = y_ref[rows]
        head_avg = (head_ones.astype(F32) * (1.0 / HEAD_DIM)).astype(BF16)
        mu_y = _dot_sel_rhs(y, head_avg)
        d = y - mu_y
        var_y = _dot_sel_rhs(d * d, head_avg, pieces=2)
        yn = d * lax.rsqrt(var_y + GN_EPS) * gnw_ref[...] + gnb_ref[...]
        bonus = _dot_sel_rhs(pre_ref[8, rows], head_ones, pieces=2)
        o_ref[0, rows] = (yn + bonus * pre_ref[6, rows]) * pre_ref[9, rows]

    pre()
    state = chain(independent(), [state_ref[g] for g in range(n_groups)])
    post()
    for g in range(n_groups):
        state_ref[g] = state[g]

    @pl.when(t == pl.num_programs(1) - 1)
    def _():
        pick = jnp.where(_iota((gw, HEAD_DIM), 0) % HEAD_DIM == _iota((gw, HEAD_DIM), 1),
                         1.0, 0.0).astype(BF16)
        for g in range(n_groups):
            wkv_ref[0, g * gw:(g + 1) * gw, :] = _dot_sel_rhs(state_ref[g], pick)


def _rwkv(p_rwkv, shift0, wkv0, prm):
    bsz, t, _ = p_rwkv.shape
    if t >= RWKV_CHUNK:
        chunk, heads_per_group, tb = RWKV_CHUNK, 4, min(t, 2 * ROW_TILE)
    else:
        chunk, heads_per_group, tb = t, N_HEADS, t
    gw = heads_per_group * HEAD_DIM
    n_groups = N_HEADS // heads_per_group
    row = lambda width: pl.BlockSpec((1, tb, width), lambda b, i: (b, i, 0))
    per_b = lambda shape: pl.BlockSpec((1,) + shape, lambda b, i: (b, 0, 0))
    vec = lambda width: _const_spec((1, width))
    o, wkv, shift = pl.pallas_call(
        functools.partial(_rwkv_kernel, chunk=chunk, heads_per_group=heads_per_group, tb=tb),
        grid=(bsz, t // tb),
        in_specs=[row(C_RWKV), per_b((1, C_RWKV)), per_b((WIDTH, HEAD_DIM)),
                  vec(C_RWKV), vec(WIDTH), _const_spec((LANES, 2 * WIDTH)), vec(WIDTH),
                  _const_spec((R_GATE, WIDTH)), vec(WIDTH), vec(WIDTH), vec(WIDTH), vec(WIDTH),
                  vec(WIDTH)],
        out_specs=[row(WIDTH), per_b((WIDTH, HEAD_DIM)), per_b((1, C_RWKV))],
        out_shape=[jax.ShapeDtypeStruct((bsz, t, WIDTH), F32),
                   jax.ShapeDtypeStruct((bsz, WIDTH, HEAD_DIM), F32),
                   jax.ShapeDtypeStruct((bsz, 1, C_RWKV), F32)],
        scratch_shapes=[pltpu.VMEM((n_groups, gw, gw), F32), pltpu.VMEM((1, C_RWKV), F32),
                        pltpu.VMEM((10, tb, WIDTH), F32), pltpu.VMEM((tb, WIDTH), F32)],
        compiler_params=_params("arbitrary", "arbitrary"),
        name="rwkv7",
    )(p_rwkv, shift0.reshape(bsz, 1, C_RWKV), wkv0.reshape(bsz, WIDTH, HEAD_DIM),
      prm["mu"], prm["w0"], prm["lora"], prm["a0"], prm["gup"], prm["kk"], prm["ka"], prm["rk"],
      prm["gnw"], prm["gnb"])
    return (o, wkv.reshape(bsz, N_HEADS, HEAD_DIM, HEAD_DIM), shift.reshape(bsz, C_RWKV))


def _merge_kernel(x_ref, oa_ref, ob_ref, ga_ref, gb_ref, wa_ref, wb_ref, wo_ref, lnw_ref, lnb_ref,
                  y_ref, *, alpha):
    merged = ga_ref[0] * _dot(oa_ref[0], wa_ref[...]) + gb_ref[0] * _dot(ob_ref[0], wb_ref[...])
    y = alpha * x_ref[0] + _dot(merged, wo_ref[...])
    y_ref[0] = _layer_norm(y, lnw_ref[...], lnb_ref[...])


def _merge(x, o_a, o_b, g_a, g_b, w_a, w_b, w_o, ln_w, ln_b, alpha):
    bsz, t, d = x.shape
    tm = min(t, ROW_TILE)
    row = lambda width: pl.BlockSpec((1, tm, width), lambda b, i: (b, i, 0))
    return pl.pallas_call(
        functools.partial(_merge_kernel, alpha=alpha),
        grid=(bsz, t // tm),
        in_specs=[row(d), row(WIDTH), row(WIDTH), row(d), row(d),
                  _const_spec(w_a.shape), _const_spec(w_b.shape), _const_spec(w_o.shape),
                  _const_spec((1, d)), _const_spec((1, d))],
        out_specs=row(d),
        out_shape=jax.ShapeDtypeStruct((bsz, t, d), F32),
        compiler_params=_params("arbitrary", "arbitrary"),
        name="merge_ln",
    )(x, o_a, o_b, g_a, g_b, w_a, w_b, w_o, ln_w, ln_b)


def _gelu_tanh(x):
    return x * (0.5 * (1.0 + jnp.tanh(math.sqrt(2.0 / math.pi) * (x + 0.044715 * (x * x * x)))))


def _ffn_kernel(x_ref, conv0_ref, wup_ref, cw_ref, cb_ref, wdn_ref, lnw_ref, lnb_ref,
                y_ref, conv_ref, tail_ref, *, alpha, tm):
    t = pl.program_id(1)
    d_ff = cb_ref.shape[-1]
    n_tail = CONV_W - 1

    @pl.when(t == 0)
    def _():
        tail_ref[...] = conv0_ref[0]

    x = x_ref[0]
    xb = x.astype(BF16)
    u = jnp.dot(xb, wup_ref[:, :d_ff], preferred_element_type=F32)
    gate = jnp.dot(xb, wup_ref[:, d_ff:], preferred_element_type=F32)
    rowi = _iota((tm, d_ff), 0)
    conv = cb_ref[...] + cw_ref[n_tail:n_tail + 1, :] * u
    for back in range(1, CONV_W):
        prev = pltpu.roll(u, back, 0)
        for j in range(back):
            prev = jnp.where(rowi == j, tail_ref[n_tail - back + j:n_tail - back + j + 1, :], prev)
        conv = conv + cw_ref[n_tail - back:n_tail - back + 1, :] * prev
    tail_ref[...] = u[tm - n_tail:tm, :]
    conv_ref[0] = u[tm - n_tail:tm, :]
    hmid = _gelu_tanh(conv) * gate
    y = alpha * x + _dot(hmid, wdn_ref[...])
    y_ref[0] = _layer_norm(y, lnw_ref[...], lnb_ref[...])


def _ffn(x, conv0, w_up, conv_w, conv_b, w_down, ln_w, ln_b, alpha):
    bsz, t, d = x.shape
    d_ff = w_down.shape[0]
    tm = min(t, ROW_TILE)
    assert tm >= CONV_W - 1
    row = lambda width: pl.BlockSpec((1, tm, width), lambda b, i: (b, i, 0))
    per_b = lambda shape: pl.BlockSpec((1,) + shape, lambda b, i: (b, 0, 0))
    return pl.pallas_call(
        functools.partial(_ffn_kernel, alpha=alpha, tm=tm),
        grid=(bsz, t // tm),
        in_specs=[row(d), per_b((CONV_W - 1, d_ff)), _const_spec(w_up.shape),
                  _const_spec((CONV_W, d_ff)), _const_spec((1, d_ff)), _const_spec(w_down.shape),
                  _const_spec((1, d)), _const_spec((1, d))],
        out_specs=[row(d), per_b((CONV_W - 1, d_ff))],
        out_shape=[jax.ShapeDtypeStruct((bsz, t, d), F32),
                   jax.ShapeDtypeStruct((bsz, CONV_W - 1, d_ff), F32)],
        scratch_shapes=[pltpu.VMEM((CONV_W - 1, d_ff), F32)],
        compiler_params=_params("arbitrary", "arbitrary"),
        name="conv_ffn_ln",
    )(x, conv0, w_up, conv_w, conv_b, w_down, ln_w, ln_b)


def _layer_params(l, w_in, b_forget, mu_shift, decay_w0, decay_up, iclr_a0, iclr_up, gate_up,
                  k_k, k_a, r_k, gn_w, gn_b, w_branch_a, w_branch_b, w_out, ln1_w, ln1_b,
                  w_up, conv_w, conv_b, w_down, ln2_w, ln2_b):
    d = w_in.shape[1]
    c_fox = 3 * WIDTH + N_HEADS
    w = w_in[l]
    w_pad = jnp.concatenate([w[:, :c_fox], jnp.zeros((d, LANES - N_HEADS), F32), w[:, c_fox:]], axis=1)
    bf = jnp.zeros((1, LANES), F32).at[0, :N_HEADS].set(b_forget[l])
    zeros = jnp.zeros((R_DECAY, WIDTH), F32)
    lora = jnp.concatenate([jnp.concatenate([decay_up[l], zeros], axis=1),
                            jnp.concatenate([zeros, iclr_up[l]], axis=1)], axis=0)
    vec = lambda z: z.reshape(1, -1)
    return dict(
        w_in=w_pad.astype(BF16), bf=bf,
        rwkv=dict(mu=vec(mu_shift[l]), w0=vec(decay_w0[l]), lora=lora.astype(BF16), a0=vec(iclr_a0[l]),
                  gup=gate_up[l].astype(BF16), kk=vec(k_k[l]), ka=vec(k_a[l]), rk=vec(r_k[l]),
                  gnw=vec(gn_w[l]), gnb=vec(gn_b[l])),
        w_a=w_branch_a[l].astype(BF16), w_b=w_branch_b[l].astype(BF16), w_o=w_out[l].astype(BF16),
        ln1_w=vec(ln1_w[l]), ln1_b=vec(ln1_b[l]),
        w_up=w_up[l].astype(BF16), conv_w=conv_w[l], conv_b=vec(conv_b[l]), w_down=w_down[l].astype(BF16),
        ln2_w=vec(ln2_w[l]), ln2_b=vec(ln2_b[l]))


def _run_group(x, layers, attend, shift0, wkv0, conv0, alpha, augment):
    bsz, t, _ = x.shape
    ks, vs, lfs, wkvs, shifts, convs = [], [], [], [], [], []
    for l, prm in enumerate(layers):
        outs = _in_proj(x, prm["w_in"], prm["bf"], augment=augment)
        if augment:
            q_aug, k, v, k_aug, vt_bf, logf, c, p_rwkv, g_a, g_b = outs
            o_a = attend(l, q_aug, k_aug, vt_bf, c)
        else:
            q_bf, k, v, logf, c, p_rwkv, g_a, g_b = outs
            o_a = attend(l, q_bf, k, v, c)
        o_b, wkv_l, shift_l = _rwkv(p_rwkv, shift0[l], wkv0[l], prm["rwkv"])
        x = _merge(x, o_a, o_b, g_a, g_b, prm["w_a"], prm["w_b"], prm["w_o"],
                   prm["ln1_w"], prm["ln1_b"], alpha)
        x, conv_l = _ffn(x, conv0[l], prm["w_up"], prm["conv_w"], prm["conv_b"], prm["w_down"],
                         prm["ln2_w"], prm["ln2_b"], alpha)
        ks.append(k.reshape(bsz, t, N_HEADS, HEAD_DIM))
        vs.append(v.reshape(bsz, t, N_HEADS, HEAD_DIM))
        lfs.append(logf)
        wkvs.append(wkv_l)
        shifts.append(shift_l)
        convs.append(conv_l)
    return (x, jnp.stack(ks), jnp.stack(vs), jnp.stack(lfs), jnp.stack(wkvs),
            jnp.stack(shifts), jnp.stack(convs))


def kernel(x_prompt, x_sample, cache_k, cache_v, cache_logf, state_wkv, state_shift, state_conv, page_table, w_in, b_forget, mu_shift, decay_w0, decay_up, iclr_a0, iclr_up, gate_up, k_k, k_a, r_k, gn_w, gn_b, w_branch_a, w_branch_b, w_out, ln1_w, ln1_b, w_up, conv_w, conv_b, w_down, ln2_w, ln2_b):
    depth = w_in.shape[0]
    alpha = (2.0 * depth) ** 0.25
    layers = [_layer_params(l, w_in, b_forget, mu_shift, decay_w0, decay_up, iclr_a0, iclr_up,
                            gate_up, k_k, k_a, r_k, gn_w, gn_b, w_branch_a, w_branch_b, w_out,
                            ln1_w, ln1_b, w_up, conv_w, conv_b, w_down, ln2_w, ln2_b)
              for l in range(depth)]

    n_pool = cache_k.shape[1]
    flat_k = cache_k.transpose(0, 1, 3, 4, 2).reshape(depth * n_pool, WIDTH, PAGE_SIZE)
    flat_v = cache_v.transpose(0, 1, 3, 4, 2).reshape(depth * n_pool, WIDTH, PAGE_SIZE)
    flat_lf = cache_logf.transpose(0, 1, 3, 2).reshape(depth * n_pool, N_HEADS, PAGE_SIZE)

    def attend_prompt(l, q_aug, k_aug, vt_bf, c):
        return _fox_prompt(q_aug, k_aug, vt_bf)

    def attend_sample(l, q_bf, k, v, c):
        return _fox_cached(q_bf, k, v, c, flat_k, flat_v, flat_lf, page_table + l * n_pool)

    bp, _, d = x_prompt.shape
    d_ff = w_down.shape[1]
    prompt = _run_group(x_prompt, layers, attend_prompt,
                        jnp.zeros((depth, bp, C_RWKV), F32),
                        jnp.zeros((depth, bp, N_HEADS, HEAD_DIM, HEAD_DIM), F32),
                        jnp.zeros((depth, bp, CONV_W - 1, d_ff), F32), alpha, True)
    sample = _run_group(x_sample, layers, attend_sample, state_shift, state_wkv, state_conv,
                        alpha, False)
    (y_p, k_p, v_p, lf_p, wkv_p, shift_p, conv_p) = prompt
    (y_s, k_s, v_s, lf_s, wkv_s, shift_s, conv_s) = sample
    return (y_p, y_s, k_p, v_p, lf_p, wkv_p, shift_p, conv_p,
            k_s, v_s, lf_s, wkv_s, shift_s, conv_s)
```

```python
import functools
import math

import jax
import jax.numpy as jnp
import numpy as np
from jax import lax
from jax.experimental import pallas as pl
from jax.experimental.pallas import tpu as pltpu

F32 = jnp.float32
BF16 = jnp.bfloat16

N_HEADS = 8
HEAD_DIM = 64
WIDTH = N_HEADS * HEAD_DIM
R_DECAY = 64
R_ICLR = 64
R_GATE = 128
C_RWKV = 3 * WIDTH + R_DECAY + R_ICLR + R_GATE
CONV_W = 3
PAGE_SIZE = 128
LN_EPS = 1e-5
GN_EPS = 64e-5
NEG_BIG = -1e30

LOG2E = math.log2(math.e)

LANES = 128
ROW_TILE = 256
ATTN_TILE = 512
ATTN_SUB = 256
V_ROWS = HEAD_DIM + 16
RWKV_CHUNK = 64
PAGES_PER_STEP = 16
VMEM_LIMIT_BYTES = 56 * 1024 * 1024

_OFF_Q, _OFF_K, _OFF_V, _OFF_F = 0, WIDTH, 2 * WIDTH, 3 * WIDTH
_OFF_RWKV = 3 * WIDTH + LANES
_OFF_GA = _OFF_RWKV + C_RWKV


def _iota(shape, dim):
    return lax.broadcasted_iota(jnp.int32, shape, dim)


def _dot(a, b):
    return jnp.dot(a.astype(BF16), b.astype(BF16), preferred_element_type=F32)


def _dot_nt(a, b):
    return lax.dot_general(a.astype(BF16), b.astype(BF16), (((1,), (1,)), ((), ())),
                           preferred_element_type=F32)


def _dot_tn(a, b):
    return lax.dot_general(a.astype(BF16), b.astype(BF16), (((0,), (0,)), ((), ())),
                           preferred_element_type=F32)


def _split3(x):
    hi = x.astype(BF16)
    r1 = x - hi.astype(F32)
    mid = r1.astype(BF16)
    lo = (r1 - mid.astype(F32)).astype(BF16)
    return hi, mid, lo


def _dot_sel_rhs(x, sel, pieces=3):
    hi, mid, lo = _split3(x)
    d = lambda p: jnp.dot(p, sel, preferred_element_type=F32)
    if pieces == 2:
        return d(hi) + d(mid)
    return d(hi) + (d(mid) + d(lo))


def _dot_sel_lhs(sel, x):
    hi, mid, lo = _split3(x)
    d = lambda p: jnp.dot(sel, p, preferred_element_type=F32)
    return d(hi) + (d(mid) + d(lo))


def _sigmoid(x):
    return 1.0 / (1.0 + jnp.exp(-x))


def _log_sigmoid(x):
    return jnp.minimum(x, 0.0) - jnp.log1p(jnp.exp(-jnp.abs(x)))


def _layer_norm(x, w, b):
    mu = jnp.mean(x, axis=-1, keepdims=True)
    d = x - mu
    var = jnp.mean(d * d, axis=-1, keepdims=True)
    return d * lax.rsqrt(var + LN_EPS) * w + b


def _params(*sem):
    return pltpu.CompilerParams(dimension_semantics=sem, vmem_limit_bytes=VMEM_LIMIT_BYTES)


def _const_spec(shape):
    zeros = (0,) * len(shape)
    return pl.BlockSpec(shape, lambda *_: zeros, pipeline_mode=pl.Buffered(1))


def _aug_constants():
    place_q = np.zeros((3 * LANES, N_HEADS * LANES), np.float32)
    place_k = np.zeros((3 * LANES, N_HEADS * LANES), np.float32)
    ones_q = np.zeros((1, N_HEADS * LANES), np.float32)
    ones_k = np.zeros((1, N_HEADS * LANES), np.float32)
    for h in range(N_HEADS):
        base = LANES * h + (HEAD_DIM if h % 2 == 0 else 0)
        for piece in range(3):
            place_q[piece * LANES + h, base + piece] = 1.0
            place_k[piece * LANES + h, base + 3 + piece] = -1.0
            ones_q[0, base + 3 + piece] = 1.0
            ones_k[0, base + piece] = 1.0
    return (jnp.asarray(place_q, BF16), jnp.asarray(place_k, BF16),
            jnp.asarray(ones_q), jnp.asarray(ones_k))


def _in_proj_kernel(x_ref, w_ref, bf_ref, *refs, tm, augment, seg):
    if augment:
        (pq_ref, pk_ref, oq_ref, ok_ref,
         q_ref, k_ref, v_ref, kaug_ref, vt_ref, logf_ref, c_ref, prw_ref, ga_ref, gb_ref, carry_ref) = refs
    else:
        (q_ref, k_ref, v_ref, logf_ref, c_ref, prw_ref, ga_ref, gb_ref, carry_ref) = refs
    t = pl.program_id(1)
    x = x_ref[0].astype(BF16)

    def proj(lo, hi):
        return jnp.dot(x, w_ref[:, lo:hi], preferred_element_type=F32)

    q = proj(_OFF_Q, _OFF_Q + WIDTH)
    k = proj(_OFF_K, _OFF_K + WIDTH)
    k_ref[0] = k
    v = proj(_OFF_V, _OFF_V + WIDTH)
    v_ref[0] = v
    if augment:
        vt = v.T.astype(BF16)
        extra = jnp.where(_iota((V_ROWS - HEAD_DIM, tm), 0) == 0, 1.0, 0.0).astype(BF16)
        vt_ref[0, 0] = jnp.concatenate(
            [z for h in range(N_HEADS) for z in (vt[HEAD_DIM * h:HEAD_DIM * (h + 1)], extra)], axis=0)
    else:
        q_ref[0] = q * HEAD_DIM ** -0.5

    z = proj(_OFF_F, _OFF_F + LANES) + bf_ref[...]
    lane = _iota((tm, LANES), 1)
    lf = jnp.where(lane < N_HEADS, _log_sigmoid(z), 0.0)
    logf_ref[0] = lf[:, :N_HEADS]

    @pl.when(t == 0)
    def _():
        carry_ref[...] = jnp.zeros_like(carry_ref)

    earlier = _iota((tm, tm), 1) <= _iota((tm, tm), 0)
    if seg is not None:
        earlier = earlier & (_iota((tm, tm), 1) // seg == _iota((tm, tm), 0) // seg)
    tril = jnp.where(earlier, 1.0, 0.0).astype(BF16)
    cs = _dot_sel_lhs(tril, lf) + carry_ref[...]
    carry_ref[...] = cs[tm - 1:tm, :]
    c_ref[0] = cs[:, :N_HEADS]

    if augment:
        pieces = jnp.concatenate(_split3(cs * LOG2E), axis=1)
        head_lane = _iota((tm, N_HEADS * LANES), 1)
        own = (head_lane % LANES) // HEAD_DIM == (head_lane // LANES) % 2
        pair = lambda z: jnp.concatenate(
            [z[:, LANES * (h // 2):LANES * (h // 2 + 1)] for h in range(N_HEADS)], axis=1)
        aug_q = jnp.dot(pieces, pq_ref[...], preferred_element_type=F32) + oq_ref[...]
        aug_k = jnp.dot(pieces, pk_ref[...], preferred_element_type=F32) + ok_ref[...]
        q_ref[0] = jnp.where(own, pair(q * (HEAD_DIM ** -0.5 * LOG2E)), aug_q).astype(BF16)
        kaug_ref[0] = jnp.where(own, pair(k), aug_k).astype(BF16)

    prw_ref[0] = proj(_OFF_RWKV, _OFF_RWKV + C_RWKV)
    d_model = ga_ref.shape[-1]
    ga_ref[0] = _sigmoid(proj(_OFF_GA, _OFF_GA + d_model))
    gb_ref[0] = _sigmoid(proj(_OFF_GA + d_model, _OFF_GA + 2 * d_model))


def _in_proj(x, w, bf, *, augment, seg=None):
    bsz, t, d = x.shape
    if not augment and bsz > 1 and t < ROW_TILE and bsz * t <= ROW_TILE:
        outs = _in_proj(x.reshape(1, bsz * t, d), w, bf, augment=False, seg=t)
        return [z.reshape(bsz, t, z.shape[-1]) for z in outs]
    tm = min(t, ROW_TILE)
    nt = t // tm
    c_all = w.shape[1]
    row = lambda width: pl.BlockSpec((1, tm, width), lambda b, i: (b, i, 0))
    in_specs = [row(d), _const_spec((d, c_all)), _const_spec((1, LANES))]
    operands = [x, w, bf]
    wide = N_HEADS * LANES
    if augment:
        per_tile = ATTN_TILE // tm
        consts = _aug_constants()
        in_specs += [_const_spec(c.shape) for c in consts]
        operands += list(consts)
        out_shape = [jax.ShapeDtypeStruct((bsz, t, wide), BF16),
                     jax.ShapeDtypeStruct((bsz, t, WIDTH), F32),
                     jax.ShapeDtypeStruct((bsz, t, WIDTH), F32),
                     jax.ShapeDtypeStruct((bsz, t, wide), BF16),
                     jax.ShapeDtypeStruct((bsz, t // ATTN_TILE, N_HEADS * V_ROWS, ATTN_TILE), BF16)]
        out_specs = [row(wide), row(WIDTH), row(WIDTH), row(wide),
                     pl.BlockSpec((1, 1, N_HEADS * V_ROWS, tm),
                                  lambda b, i: (b, i // per_tile, 0, i % per_tile))]
    else:
        out_shape = [jax.ShapeDtypeStruct((bsz, t, WIDTH), F32),
                     jax.ShapeDtypeStruct((bsz, t, WIDTH), F32),
                     jax.ShapeDtypeStruct((bsz, t, WIDTH), F32)]
        out_specs = [row(WIDTH), row(WIDTH), row(WIDTH)]
    out_shape += [jax.ShapeDtypeStruct((bsz, t, N_HEADS), F32),
                  jax.ShapeDtypeStruct((bsz, t, N_HEADS), F32),
                  jax.ShapeDtypeStruct((bsz, t, C_RWKV), F32),
                  jax.ShapeDtypeStruct((bsz, t, d), F32),
                  jax.ShapeDtypeStruct((bsz, t, d), F32)]
    out_specs += [row(N_HEADS), row(N_HEADS), row(C_RWKV), row(d), row(d)]
    return pl.pallas_call(
        functools.partial(_in_proj_kernel, tm=tm, augment=augment, seg=seg),
        grid=(bsz, nt),
        in_specs=in_specs,
        out_specs=out_specs,
        out_shape=out_shape,
        scratch_shapes=[pltpu.VMEM((1, LANES), F32)],
        compiler_params=_params("arbitrary", "arbitrary"),
        name="in_proj",
    )(*operands)


def _fox_prompt_kernel(q_ref, k_ref, vt_ref, o_ref, s_ref, m_ref, acc_ref, *, tile):
    i = pl.program_id(2)
    heads = (0, 1)
    key_pos = _iota((ATTN_SUB, tile), 0)
    query_pos = _iota((ATTN_SUB, tile), 1)

    pairs = [(sub, h) for sub in range(tile // ATTN_SUB) for h in heads]

    def score(j, slot):
        start = pl.multiple_of(j * tile, tile)
        for n, (sub, h) in enumerate(pairs):
            lanes = slice(LANES * h, LANES * (h + 1))
            s_ref[slot, n] = _dot_nt(k_ref[0, pl.ds(start + sub * ATTN_SUB, ATTN_SUB), lanes],
                                     q_ref[0, :, lanes])

    def consume(j, slot, masked):
        carry = [(m_ref[h], acc_ref[h]) for h in heads]
        probs = []
        scale = []
        for n, (sub, h) in enumerate(pairs):
            s = s_ref[slot, n]
            if masked:
                s = jnp.where(key_pos + sub * ATTN_SUB <= query_pos, s, NEG_BIG)
            m, acc = carry[h]
            m_new = jnp.maximum(m, jnp.max(s, axis=0, keepdims=True))
            probs.append(jnp.exp2(s - m_new).astype(BF16))
            scale.append(jnp.exp2(m - m_new))
            carry[h] = (m_new, acc)
        for n, (sub, h) in enumerate(pairs):
            m_new, acc = carry[h]
            vt = vt_ref[0, j, V_ROWS * h:V_ROWS * (h + 1), sub * ATTN_SUB:(sub + 1) * ATTN_SUB]
            carry[h] = (m_new, scale[n] * acc + jnp.dot(vt, probs[n], preferred_element_type=F32))
        for h in heads:
            m_ref[h], acc_ref[h] = carry[h]

    for h in heads:
        m_ref[h] = jnp.full((1, tile), NEG_BIG, F32)
        acc_ref[h] = jnp.zeros((V_ROWS, tile), F32)

    score(0, 0)

    def two_tiles(jj, _):
        j = 2 * jj
        score(j + 1, 1)
        consume(j, 0, False)
        score(j + 2, 0)
        consume(j + 1, 1, False)
        return 0

    lax.fori_loop(0, i // 2, two_tiles, 0)
    odd = i % 2 == 1

    @pl.when(odd)
    def _():
        score(i, 1)
        consume(i - 1, 0, False)
        consume(i, 1, True)

    @pl.when(jnp.logical_not(odd))
    def _():
        consume(i, 0, True)

    o_ref[0] = jnp.concatenate(
        [acc_ref[h, :HEAD_DIM] / acc_ref[h, HEAD_DIM:HEAD_DIM + 1] for h in heads], axis=0).T


def _fox_prompt(q_aug, k_aug, vt_bf):
    bsz, t, _ = q_aug.shape
    tile = vt_bf.shape[-1]
    nq = t // tile
    n_pairs = N_HEADS // 2
    return pl.pallas_call(
        functools.partial(_fox_prompt_kernel, tile=tile),
        grid=(bsz, n_pairs, nq),
        in_specs=[pl.BlockSpec((1, tile, 2 * LANES), lambda b, hp, i: (b, i, hp)),
                  pl.BlockSpec((1, t, 2 * LANES), lambda b, hp, i: (b, 0, hp)),
                  pl.BlockSpec((1, nq, 2 * V_ROWS, tile), lambda b, hp, i: (b, 0, hp, 0))],
        out_specs=pl.BlockSpec((1, tile, LANES), lambda b, hp, i: (b, i, hp)),
        out_shape=jax.ShapeDtypeStruct((bsz, t, WIDTH), F32),
        scratch_shapes=[pltpu.VMEM((2, 2 * (tile // ATTN_SUB), ATTN_SUB, tile), F32),
                        pltpu.VMEM((2, 1, tile), F32), pltpu.VMEM((2, V_ROWS, tile), F32)],
        compiler_params=_params("arbitrary", "arbitrary", "arbitrary"),
        name="fox_prompt",
    )(q_aug, k_aug, vt_bf)


def _fox_cached_kernel(pt_ref, q_ref, knew_ref, vnew_ref, cq_ref, ck_ref, later_ref, *refs, n_group):
    del pt_ref
    kp = refs[:n_group]
    vp = refs[n_group:2 * n_group]
    lp = refs[2 * n_group:3 * n_group]
    o_ref, m_ref, l_ref, acc_ref, carry_ref, both_ref = refs[3 * n_group:]
    step = pl.program_id(1)
    t_new = q_ref.shape[1]
    rows = N_HEADS * t_new
    own_head = _iota((rows, WIDTH), 0) // t_new == _iota((rows, WIDTH), 1) // HEAD_DIM
    qx = jnp.where(own_head, jnp.concatenate([q_ref[0]] * N_HEADS, axis=0), 0.0).astype(BF16)

    @pl.when(step == 0)
    def _():
        s = _dot_nt(qx, knew_ref[0])
        s = s + (cq_ref[0] - ck_ref[0])
        qpos = _iota((rows, t_new), 0) % t_new
        s = jnp.where(_iota((rows, t_new), 1) <= qpos, s, NEG_BIG)
        m = jnp.max(s, axis=1, keepdims=True)
        p = jnp.exp(s - m)
        m_ref[...] = m
        l_ref[...] = jnp.sum(p, axis=1, keepdims=True)
        acc_ref[...] = _dot(p, vnew_ref[0])
        carry_ref[...] = jnp.broadcast_to(cq_ref[0], carry_ref.shape)

    groups = range(n_group)
    raw = [_dot(qx, kp[g][0]) for g in groups]
    lf_t = jnp.concatenate([lp[g][0] for g in groups], axis=0)
    both_ref[...] = _dot_sel_rhs(lf_t, later_ref[...])
    carry = carry_ref[...]
    scores = []
    for g in groups:
        both = jnp.concatenate(
            [jnp.broadcast_to(both_ref[g * N_HEADS + h:g * N_HEADS + h + 1, :], (t_new, 2 * PAGE_SIZE))
             for h in range(N_HEADS)], axis=0)
        scores.append(raw[g] + (both[:, :PAGE_SIZE] + carry))
        carry = carry + both[:, PAGE_SIZE:]
    carry_ref[...] = carry

    smax = scores[0]
    for s in scores[1:]:
        smax = jnp.maximum(smax, s)
    m_old = m_ref[...]
    m_new = jnp.maximum(m_old, jnp.max(smax, axis=1, keepdims=True))
    alpha = jnp.exp(m_old - m_new)
    l = alpha * l_ref[...]
    acc = alpha * acc_ref[...]
    probs = [jnp.exp(scores[g] - m_new) for g in groups]
    for g in groups:
        l = l + jnp.sum(probs[g], axis=1, keepdims=True)
        acc = acc + _dot_nt(probs[g], vp[g][0])
    m_ref[...] = m_new
    l_ref[...] = l
    acc_ref[...] = acc

    @pl.when(step == pl.num_programs(1) - 1)
    def _():
        o = jnp.where(own_head, acc / l, 0.0)
        out = o[0:t_new]
        for h in range(1, N_HEADS):
            out = out + o[h * t_new:(h + 1) * t_new]
        o_ref[0] = out


def _fox_cached(q_bf, k_new, v_new, c_new, cache_k, cache_v, cache_logf, pages):
    bsz, t_new, _ = q_bf.shape
    n_pages = pages.shape[1]
    n_group = math.gcd(PAGES_PER_STEP, n_pages)
    n_steps = n_pages // n_group
    rows = N_HEADS * t_new
    c_q = c_new.transpose(0, 2, 1).reshape(bsz, rows, 1)
    c_k = jnp.repeat(c_new.transpose(0, 2, 1), t_new, axis=1)

    def page_map(g, n_trailing):
        zeros = (0,) * n_trailing
        return lambda b, s, pt: (pt[b, n_pages - 1 - (s * n_group + g)],) + zeros

    def per_b(shape):
        zeros = (0,) * len(shape)
        return pl.BlockSpec((1,) + shape, lambda b, s, pt: (b,) + zeros)

    keys = np.arange(PAGE_SIZE)
    later = np.concatenate([(keys[:, None] > keys[None, :]).astype(np.float32),
                            np.ones((PAGE_SIZE, PAGE_SIZE), np.float32)], axis=1)
    later = jnp.asarray(later, BF16)

    tok_blk = (t_new, WIDTH)
    in_specs = [per_b(tok_blk), per_b(tok_blk), per_b(tok_blk), per_b((rows, 1)), per_b((rows, t_new)),
                pl.BlockSpec(later.shape, lambda b, s, pt: (0, 0), pipeline_mode=pl.Buffered(1))]
    kv_blk = (1, WIDTH, PAGE_SIZE)
    in_specs += [pl.BlockSpec(kv_blk, page_map(g, 2)) for g in range(n_group)]
    in_specs += [pl.BlockSpec(kv_blk, page_map(g, 2)) for g in range(n_group)]
    in_specs += [pl.BlockSpec((1, N_HEADS, PAGE_SIZE), page_map(g, 2)) for g in range(n_group)]
    return pl.pallas_call(
        functools.partial(_fox_cached_kernel, n_group=n_group),
        grid_spec=pltpu.PrefetchScalarGridSpec(
            num_scalar_prefetch=1,
            grid=(bsz, n_steps),
            in_specs=in_specs,
            out_specs=per_b(tok_blk),
            scratch_shapes=[pltpu.VMEM((rows, 1), F32), pltpu.VMEM((rows, 1), F32),
                            pltpu.VMEM((rows, WIDTH), F32), pltpu.VMEM((rows, PAGE_SIZE), F32),
                            pltpu.VMEM((n_group * N_HEADS, later.shape[1]), F32)]),
        out_shape=jax.ShapeDtypeStruct((bsz,) + tok_blk, F32),
        compiler_params=_params("arbitrary", "arbitrary"),
        name="fox_cached",
    )(pages, q_bf, k_new, v_new, c_q, c_k, later,
      *([cache_k] * n_group), *([cache_v] * n_group), *([cache_logf] * n_group))


def _unit_lower_inverse(n_mats, eye, blk_of_row, blk_of_col, chunk):
    base = 8
    same = lambda size: (blk_of_row // (size // base)) == (blk_of_col // (size // base))
    ms = [jnp.where(same(base), n, 0.0) for n in n_mats]
    xs = [eye + m for m in ms]
    for _ in range(2):
        ms = [_dot(m, m) for m in ms]
        xs = [x + _dot(x, m) for x, m in zip(xs, ms)]
    size = base
    while size < chunk:
        ring = same(2 * size) & jnp.logical_not(same(size))
        half = [_dot(x, jnp.where(ring, n, 0.0)) for x, n in zip(xs, n_mats)]
        xs = [x + _dot(h, x) for x, h in zip(xs, half)]
        size *= 2
    return xs


def _rwkv_kernel(p_ref, shift0_ref, wkv0_ref, mu_ref, w0_ref, lora_ref, a0_ref, gup_ref,
                 kk_ref, ka_ref, rk_ref, gnw_ref, gnb_ref,
                 o_ref, wkv_ref, shift_ref,
                 state_ref, prev_ref, pre_ref, y_ref, *, chunk, heads_per_group, tb):
    t = pl.program_id(1)
    gw = heads_per_group * HEAD_DIM
    n_groups = N_HEADS // heads_per_group
    gl = heads_per_group * chunk
    n_chunks = tb // chunk

    row_head = _iota((gw, gw), 0) // HEAD_DIM
    col_head = _iota((gw, gw), 1) // HEAD_DIM
    state_mask = row_head == col_head

    @pl.when(t == 0)
    def _():
        prev_ref[...] = shift0_ref[0]
        tile_lanes = jnp.where(_iota((HEAD_DIM, gw), 0) == _iota((HEAD_DIM, gw), 1) % HEAD_DIM,
                               1.0, 0.0).astype(BF16)
        for g in range(n_groups):
            s0 = wkv0_ref[0, g * gw:(g + 1) * gw, :]
            state_ref[g] = jnp.where(state_mask, _dot_sel_rhs(s0, tile_lanes), 0.0)

    p = p_ref[0]
    first_row = _iota((tb, C_RWKV), 0) == 0
    shifted = jnp.where(first_row, prev_ref[...], pltpu.roll(p, 1, 0))
    prev_ref[...] = p[tb - 1:tb, :]
    shift_ref[0] = p[tb - 1:tb, :]
    seg_r = _iota((WIDTH, WIDTH), 0) // HEAD_DIM
    seg_c = _iota((WIDTH, WIDTH), 1) // HEAD_DIM
    head_ones = jnp.where(seg_r == seg_c, 1.0, 0.0).astype(BF16)

    def pre():
        rows = slice(0, tb)
        xs = p + mu_ref[...] * (shifted - p)
        r = xs[:, 0:WIDTH]
        k = xs[:, WIDTH:2 * WIDTH]
        v = xs[:, 2 * WIDTH:3 * WIDTH]
        wa = xs[:, 3 * WIDTH:3 * WIDTH + LANES]
        gd = xs[:, 3 * WIDTH + LANES:]
        lane = _iota((tb, LANES), 1)
        lora = _dot(jnp.where(lane < R_DECAY, jnp.tanh(wa), wa), lora_ref[...])
        w_raw = _log_sigmoid(w0_ref[...] + lora[:, :WIDTH]) - 0.5
        lw = -jnp.exp(w_raw)
        iclr = _sigmoid(a0_ref[...] + lora[:, WIDTH:])
        kk = k * kk_ref[...]
        kk_norm = jnp.sqrt(_dot_sel_rhs(kk * kk, head_ones, pieces=2))
        kk = kk / jnp.maximum(kk_norm, 1e-12)
        k2 = k * (1.0 + (iclr - 1.0) * ka_ref[...])
        a = -kk
        b = kk * iclr
        sq = (tb, tb)
        incl = jnp.where((_iota(sq, 0) // chunk == _iota(sq, 1) // chunk) & (_iota(sq, 1) <= _iota(sq, 0)),
                         1.0, 0.0).astype(BF16)
        cum = _dot_sel_lhs(incl, lw)
        cum_end = jnp.concatenate(
            [jnp.broadcast_to(cum[(c + 1) * chunk - 1:(c + 1) * chunk, :], (chunk, WIDTH))
             for c in range(n_chunks)], axis=0)
        pre_ref[0, rows] = a * jnp.exp(cum - lw)
        pre_ref[1, rows] = r * jnp.exp(cum)
        pre_ref[2, rows] = b * jnp.exp(-cum)
        pre_ref[3, rows] = k2 * jnp.exp(-cum)
        pre_ref[4, rows] = b * jnp.exp(cum_end - cum)
        pre_ref[5, rows] = k2 * jnp.exp(cum_end - cum)
        pre_ref[6, rows] = v
        pre_ref[7, rows] = jnp.exp(cum_end)
        pre_ref[8, rows] = r * k2 * rk_ref[...]
        pre_ref[9, rows] = _dot(_sigmoid(gd), gup_ref[...])

    e_row_head = _iota((gl, gw), 0) // chunk
    e_col_head = _iota((gl, gw), 1) // HEAD_DIM
    expand_mask = e_row_head == e_col_head
    sq_r = _iota((gl, gl), 0)
    sq_c = _iota((gl, gl), 1)
    same_head = (sq_r // chunk) == (sq_c // chunk)
    strict = same_head & (sq_c < sq_r)
    lower = same_head & (sq_c <= sq_r)
    eye = jnp.where(sq_r == sq_c, 1.0, 0.0)
    blk8_r = sq_r // 8
    blk8_c = sq_c // 8

    def expand(x):
        return jnp.where(expand_mask, jnp.concatenate([x] * heads_per_group, axis=0), 0.0)

    def collapse(x):
        out = x[0:chunk]
        for h in range(1, heads_per_group):
            out = out + x[h * chunk:(h + 1) * chunk]
        return out

    def tile_rows(x):
        return jnp.concatenate([x] * heads_per_group, axis=0)

    def ld(idx, c, g, n_rows=chunk):
        return pre_ref[idx, c * chunk:c * chunk + n_rows, g * gw:(g + 1) * gw]

    def independent():
        pairs = [(c, g) for c in range(n_chunks) for g in range(n_groups)]
        ax = {cg: expand(ld(0, *cg)).astype(BF16) for cg in pairs}
        rx = {cg: expand(ld(1, *cg)).astype(BF16) for cg in pairs}
        bt = {cg: tile_rows(ld(2, *cg)).astype(BF16) for cg in pairs}
        kt = {cg: tile_rows(ld(3, *cg)).astype(BF16) for cg in pairs}
        vx = {cg: expand(ld(6, *cg)).astype(BF16) for cg in pairs}
        n_mats = [jnp.where(strict, _dot_nt(ax[cg], bt[cg]), 0.0) for cg in pairs]
        a_ak = {cg: jnp.where(strict, _dot_nt(ax[cg], kt[cg]), 0.0).astype(BF16) for cg in pairs}
        a_rb = {cg: jnp.where(lower, _dot_nt(rx[cg], bt[cg]), 0.0).astype(BF16) for cg in pairs}
        a_rk = {cg: jnp.where(lower, _dot_nt(rx[cg], kt[cg]), 0.0).astype(BF16) for cg in pairs}
        t_inv = dict(zip(pairs, (x.astype(BF16)
                                 for x in _unit_lower_inverse(n_mats, eye, blk8_r, blk8_c, chunk))))
        akv = {cg: _dot(a_ak[cg], vx[cg]) for cg in pairs}
        rkv = {cg: _dot(a_rk[cg], vx[cg]) for cg in pairs}
        return t_inv, a_rb, akv, rkv

    def chain(mats, state):
        t_inv, a_rb, akv, rkv = mats
        for c in range(n_chunks):
            ux = {}
            for g in range(n_groups):
                rhs = expand(_dot_nt(ld(0, c, g), state[g])) + akv[c, g]
                ux[g] = _dot(t_inv[c, g], rhs)
            for g in range(n_groups):
                y = _dot_nt(ld(1, c, g), state[g]) + collapse(_dot(a_rb[c, g], ux[g]) + rkv[c, g])
                y_ref[c * chunk:(c + 1) * chunk, g * gw:(g + 1) * gw] = y
            for g in range(n_groups):
                uv = jnp.concatenate([collapse(ux[g]), ld(6, c, g)], axis=0)
                bk = jnp.concatenate([ld(4, c, g), ld(5, c, g)], axis=0)
                state[g] = state[g] * ld(7, c, g, 1) + jnp.where(state_mask, _dot_tn(uv, bk), 0.0)
        return state

    def post():
        rows = slice(0, tb)
        y = y_ref[rows]
        head_avg = (head_ones.astype(F32) * (1.0 / HEAD_DIM)).astype(BF16)
        mu_y = _dot_sel_rhs(y, head_avg)
        d = y - mu_y
        var_y = _dot_sel_rhs(d * d, head_avg, pieces=2)
        yn = d * lax.rsqrt(var_y + GN_EPS) * gnw_ref[...] + gnb_ref[...]
        bonus = _dot_sel_rhs(pre_ref[8, rows], head_ones, pieces=2)
        o_ref[0, rows] = (yn + bonus * pre_ref[6, rows]) * pre_ref[9, rows]

    pre()
    state = chain(independent(), [state_ref[g] for g in range(n_groups)])
    post()
    for g in range(n_groups):
        state_ref[g] = state[g]

    @pl.when(t == pl.num_programs(1) - 1)
    def _():
        pick = jnp.where(_iota((gw, HEAD_DIM), 0) % HEAD_DIM == _iota((gw, HEAD_DIM), 1),
                         1.0, 0.0).astype(BF16)
        for g in range(n_groups):
            wkv_ref[0, g * gw:(g + 1) * gw, :] = _dot_sel_rhs(state_ref[g], pick)


def _rwkv(p_rwkv, shift0, wkv0, prm):
    bsz, t, _ = p_rwkv.shape
    if t >= RWKV_CHUNK:
        chunk, heads_per_group, tb = RWKV_CHUNK, 4, min(t, 2 * ROW_TILE)
    else:
        chunk, heads_per_group, tb = t, N_HEADS, t
    gw = heads_per_group * HEAD_DIM
    n_groups = N_HEADS // heads_per_group
    row = lambda width: pl.BlockSpec((1, tb, width), lambda b, i: (b, i, 0))
    per_b = lambda shape: pl.BlockSpec((1,) + shape, lambda b, i: (b, 0, 0))
    vec = lambda width: _const_spec((1, width))
    o, wkv, shift = pl.pallas_call(
        functools.partial(_rwkv_kernel, chunk=chunk, heads_per_group=heads_per_group, tb=tb),
        grid=(bsz, t // tb),
        in_specs=[row(C_RWKV), per_b((1, C_RWKV)), per_b((WIDTH, HEAD_DIM)),
                  vec(C_RWKV), vec(WIDTH), _const_spec((LANES, 2 * WIDTH)), vec(WIDTH),
                  _const_spec((R_GATE, WIDTH)), vec(WIDTH), vec(WIDTH), vec(WIDTH), vec(WIDTH),
                  vec(WIDTH)],
        out_specs=[row(WIDTH), per_b((WIDTH, HEAD_DIM)), per_b((1, C_RWKV))],
        out_shape=[jax.ShapeDtypeStruct((bsz, t, WIDTH), F32),
                   jax.ShapeDtypeStruct((bsz, WIDTH, HEAD_DIM), F32),
                   jax.ShapeDtypeStruct((bsz, 1, C_RWKV), F32)],
        scratch_shapes=[pltpu.VMEM((n_groups, gw, gw), F32), pltpu.VMEM((1, C_RWKV), F32),
                        pltpu.VMEM((10, tb, WIDTH), F32), pltpu.VMEM((tb, WIDTH), F32)],
        compiler_params=_params("arbitrary", "arbitrary"),
        name="rwkv7",
    )(p_rwkv, shift0.reshape(bsz, 1, C_RWKV), wkv0.reshape(bsz, WIDTH, HEAD_DIM),
      prm["mu"], prm["w0"], prm["lora"], prm["a0"], prm["gup"], prm["kk"], prm["ka"], prm["rk"],
      prm["gnw"], prm["gnb"])
    return (o, wkv.reshape(bsz, N_HEADS, HEAD_DIM, HEAD_DIM), shift.reshape(bsz, C_RWKV))


def _merge_kernel(x_ref, oa_ref, ob_ref, ga_ref, gb_ref, wa_ref, wb_ref, wo_ref, lnw_ref, lnb_ref,
                  y_ref, *, alpha):
    merged = ga_ref[0] * _dot(oa_ref[0], wa_ref[...]) + gb_ref[0] * _dot(ob_ref[0], wb_ref[...])
    y = alpha * x_ref[0] + _dot(merged, wo_ref[...])
    y_ref[0] = _layer_norm(y, lnw_ref[...], lnb_ref[...])


def _merge(x, o_a, o_b, g_a, g_b, w_a, w_b, w_o, ln_w, ln_b, alpha):
    bsz, t, d = x.shape
    if t < ROW_TILE and bsz > 1 and bsz * t <= ROW_TILE:
        flat = lambda z: z.reshape(1, bsz * t, z.shape[-1])
        y = _merge(flat(x), flat(o_a), flat(o_b), flat(g_a), flat(g_b), w_a, w_b, w_o, ln_w, ln_b, alpha)
        return y.reshape(bsz, t, d)
    tm = min(t, ROW_TILE)
    row = lambda width: pl.BlockSpec((1, tm, width), lambda b, i: (b, i, 0))
    return pl.pallas_call(
        functools.partial(_merge_kernel, alpha=alpha),
        grid=(bsz, t // tm),
        in_specs=[row(d), row(WIDTH), row(WIDTH), row(d), row(d),
                  _const_spec(w_a.shape), _const_spec(w_b.shape), _const_spec(w_o.shape),
                  _const_spec((1, d)), _const_spec((1, d))],
        out_specs=row(d),
        out_shape=jax.ShapeDtypeStruct((bsz, t, d), F32),
        compiler_params=_params("arbitrary", "arbitrary"),
        name="merge_ln",
    )(x, o_a, o_b, g_a, g_b, w_a, w_b, w_o, ln_w, ln_b)


def _gelu_tanh(x):
    return x * (0.5 * (1.0 + jnp.tanh(math.sqrt(2.0 / math.pi) * (x + 0.044715 * (x * x * x)))))


def _ffn_kernel(x_ref, *refs, alpha, tm, seg):
    if seg is None:
        (conv0_ref, wup_ref, cw_ref, cb_ref, wdn_ref, lnw_ref, lnb_ref, y_ref, conv_ref, tail_ref) = refs
    else:
        (back1_ref, back2_ref, wup_ref, cw_ref, cb_ref, wdn_ref, lnw_ref, lnb_ref, y_ref, conv_ref) = refs
    d_ff = cb_ref.shape[-1]
    n_tail = CONV_W - 1

    x = x_ref[0]
    xb = x.astype(BF16)
    u = jnp.dot(xb, wup_ref[:, :d_ff], preferred_element_type=F32)
    gate = jnp.dot(xb, wup_ref[:, d_ff:], preferred_element_type=F32)
    rowi = _iota((tm, d_ff), 0)
    conv = cb_ref[...] + cw_ref[n_tail:n_tail + 1, :] * u
    if seg is None:
        @pl.when(pl.program_id(1) == 0)
        def _():
            tail_ref[...] = conv0_ref[0]

        for back in range(1, CONV_W):
            prev = pltpu.roll(u, back, 0)
            for j in range(back):
                prev = jnp.where(rowi == j, tail_ref[n_tail - back + j:n_tail - back + j + 1, :], prev)
            conv = conv + cw_ref[n_tail - back:n_tail - back + 1, :] * prev
        tail_ref[...] = u[tm - n_tail:tm, :]
        conv_ref[0] = u[tm - n_tail:tm, :]
    else:
        pos = rowi % seg
        for back, back_ref in ((1, back1_ref), (2, back2_ref)):
            prev = jnp.where(pos >= back, pltpu.roll(u, back, 0), back_ref[0])
            conv = conv + cw_ref[n_tail - back:n_tail - back + 1, :] * prev
        conv_ref[0] = u
    hmid = _gelu_tanh(conv) * gate
    y = alpha * x + _dot(hmid, wdn_ref[...])
    y_ref[0] = _layer_norm(y, lnw_ref[...], lnb_ref[...])


def _ffn_short(x, conv0, w_up, conv_w, conv_b, w_down, ln_w, ln_b, alpha):
    bsz, t, d = x.shape
    d_ff = w_down.shape[0]
    rows = bsz * t
    assert CONV_W == 3 and t >= CONV_W - 1 and rows <= ROW_TILE
    zeros = jnp.zeros((bsz, t, d_ff), F32)
    back1 = zeros.at[:, 0].set(conv0[:, 1]).reshape(1, rows, d_ff)
    back2 = zeros.at[:, 0].set(conv0[:, 0]).at[:, 1].set(conv0[:, 1]).reshape(1, rows, d_ff)
    whole = lambda width: pl.BlockSpec((1, rows, width), lambda i: (0, 0, 0))
    y, u = pl.pallas_call(
        functools.partial(_ffn_kernel, alpha=alpha, tm=rows, seg=t),
        grid=(1,),
        in_specs=[whole(d), whole(d_ff), whole(d_ff), _const_spec(w_up.shape),
                  _const_spec((CONV_W, d_ff)), _const_spec((1, d_ff)), _const_spec(w_down.shape),
                  _const_spec((1, d)), _const_spec((1, d))],
        out_specs=[whole(d), whole(d_ff)],
        out_shape=[jax.ShapeDtypeStruct((1, rows, d), F32), jax.ShapeDtypeStruct((1, rows, d_ff), F32)],
        compiler_params=_params("arbitrary"),
        name="conv_ffn_ln",
    )(x.reshape(1, rows, d), back1, back2, w_up, conv_w, conv_b, w_down, ln_w, ln_b)
    return y.reshape(bsz, t, d), u.reshape(bsz, t, d_ff)[:, t - (CONV_W - 1):]


def _ffn(x, conv0, w_up, conv_w, conv_b, w_down, ln_w, ln_b, alpha):
    bsz, t, d = x.shape
    d_ff = w_down.shape[0]
    if t < ROW_TILE and bsz * t <= ROW_TILE:
        return _ffn_short(x, conv0, w_up, conv_w, conv_b, w_down, ln_w, ln_b, alpha)
    tm = min(t, ROW_TILE)
    assert tm >= CONV_W - 1
    row = lambda width: pl.BlockSpec((1, tm, width), lambda b, i: (b, i, 0))
    per_b = lambda shape: pl.BlockSpec((1,) + shape, lambda b, i: (b, 0, 0))
    return pl.pallas_call(
        functools.partial(_ffn_kernel, alpha=alpha, tm=tm, seg=None),
        grid=(bsz, t // tm),
        in_specs=[row(d), per_b((CONV_W - 1, d_ff)), _const_spec(w_up.shape),
                  _const_spec((CONV_W, d_ff)), _const_spec((1, d_ff)), _const_spec(w_down.shape),
                  _const_spec((1, d)), _const_spec((1, d))],
        out_specs=[row(d), per_b((CONV_W - 1, d_ff))],
        out_shape=[jax.ShapeDtypeStruct((bsz, t, d), F32),
                   jax.ShapeDtypeStruct((bsz, CONV_W - 1, d_ff), F32)],
        scratch_shapes=[pltpu.VMEM((CONV_W - 1, d_ff), F32)],
        compiler_params=_params("arbitrary", "arbitrary"),
        name="conv_ffn_ln",
    )(x, conv0, w_up, conv_w, conv_b, w_down, ln_w, ln_b)


def _layer_params(l, w_in, b_forget, mu_shift, decay_w0, decay_up, iclr_a0, iclr_up, gate_up,
                  k_k, k_a, r_k, gn_w, gn_b, w_branch_a, w_branch_b, w_out, ln1_w, ln1_b,
                  w_up, conv_w, conv_b, w_down, ln2_w, ln2_b):
    d = w_in.shape[1]
    c_fox = 3 * WIDTH + N_HEADS
    w = w_in[l]
    w_pad = jnp.concatenate([w[:, :c_fox], jnp.zeros((d, LANES - N_HEADS), F32), w[:, c_fox:]], axis=1)
    bf = jnp.zeros((1, LANES), F32).at[0, :N_HEADS].set(b_forget[l])
    zeros = jnp.zeros((R_DECAY, WIDTH), F32)
    lora = jnp.concatenate([jnp.concatenate([decay_up[l], zeros], axis=1),
                            jnp.concatenate([zeros, iclr_up[l]], axis=1)], axis=0)
    vec = lambda z: z.reshape(1, -1)
    return dict(
        w_in=w_pad.astype(BF16), bf=bf,
        rwkv=dict(mu=vec(mu_shift[l]), w0=vec(decay_w0[l]), lora=lora.astype(BF16), a0=vec(iclr_a0[l]),
                  gup=gate_up[l].astype(BF16), kk=vec(k_k[l]), ka=vec(k_a[l]), rk=vec(r_k[l]),
                  gnw=vec(gn_w[l]), gnb=vec(gn_b[l])),
        w_a=w_branch_a[l].astype(BF16), w_b=w_branch_b[l].astype(BF16), w_o=w_out[l].astype(BF16),
        ln1_w=vec(ln1_w[l]), ln1_b=vec(ln1_b[l]),
        w_up=w_up[l].astype(BF16), conv_w=conv_w[l], conv_b=vec(conv_b[l]), w_down=w_down[l].astype(BF16),
        ln2_w=vec(ln2_w[l]), ln2_b=vec(ln2_b[l]))


def _run_group(x, layers, attend, shift0, wkv0, conv0, alpha, augment):
    bsz, t, _ = x.shape
    ks, vs, lfs, wkvs, shifts, convs = [], [], [], [], [], []
    for l, prm in enumerate(layers):
        outs = _in_proj(x, prm["w_in"], prm["bf"], augment=augment)
        if augment:
            q_aug, k, v, k_aug, vt_bf, logf, c, p_rwkv, g_a, g_b = outs
            o_a = attend(l, q_aug, k_aug, vt_bf, c)
        else:
            q_bf, k, v, logf, c, p_rwkv, g_a, g_b = outs
            o_a = attend(l, q_bf, k, v, c)
        o_b, wkv_l, shift_l = _rwkv(p_rwkv, shift0[l], wkv0[l], prm["rwkv"])
        x = _merge(x, o_a, o_b, g_a, g_b, prm["w_a"], prm["w_b"], prm["w_o"],
                   prm["ln1_w"], prm["ln1_b"], alpha)
        x, conv_l = _ffn(x, conv0[l], prm["w_up"], prm["conv_w"], prm["conv_b"], prm["w_down"],
                         prm["ln2_w"], prm["ln2_b"], alpha)
        ks.append(k.reshape(bsz, t, N_HEADS, HEAD_DIM))
        vs.append(v.reshape(bsz, t, N_HEADS, HEAD_DIM))
        lfs.append(logf)
        wkvs.append(wkv_l)
        shifts.append(shift_l)
        convs.append(conv_l)
    return (x, jnp.stack(ks), jnp.stack(vs), jnp.stack(lfs), jnp.stack(wkvs),
            jnp.stack(shifts), jnp.stack(convs))


def kernel(x_prompt, x_sample, cache_k, cache_v, cache_logf, state_wkv, state_shift, state_conv, page_table, w_in, b_forget, mu_shift, decay_w0, decay_up, iclr_a0, iclr_up, gate_up, k_k, k_a, r_k, gn_w, gn_b, w_branch_a, w_branch_b, w_out, ln1_w, ln1_b, w_up, conv_w, conv_b, w_down, ln2_w, ln2_b):
    depth = w_in.shape[0]
    alpha = (2.0 * depth) ** 0.25
    layers = [_layer_params(l, w_in, b_forget, mu_shift, decay_w0, decay_up, iclr_a0, iclr_up,
                            gate_up, k_k, k_a, r_k, gn_w, gn_b, w_branch_a, w_branch_b, w_out,
                            ln1_w, ln1_b, w_up, conv_w, conv_b, w_down, ln2_w, ln2_b)
              for l in range(depth)]

    n_pool = cache_k.shape[1]
    flat_k = cache_k.transpose(0, 1, 3, 4, 2).reshape(depth * n_pool, WIDTH, PAGE_SIZE)
    flat_v = cache_v.transpose(0, 1, 3, 4, 2).reshape(depth * n_pool, WIDTH, PAGE_SIZE)
    flat_lf = cache_logf.transpose(0, 1, 3, 2).reshape(depth * n_pool, N_HEADS, PAGE_SIZE)

    def attend_prompt(l, q_aug, k_aug, vt_bf, c):
        return _fox_prompt(q_aug, k_aug, vt_bf)

    def attend_sample(l, q_bf, k, v, c):
        return _fox_cached(q_bf, k, v, c, flat_k, flat_v, flat_lf, page_table + l * n_pool)

    bp, _, d = x_prompt.shape
    d_ff = w_down.shape[1]
    prompt = _run_group(x_prompt, layers, attend_prompt,
                        jnp.zeros((depth, bp, C_RWKV), F32),
                        jnp.zeros((depth, bp, N_HEADS, HEAD_DIM, HEAD_DIM), F32),
                        jnp.zeros((depth, bp, CONV_W - 1, d_ff), F32), alpha, True)
    sample = _run_group(x_sample, layers, attend_sample, state_shift, state_wkv, state_conv,
                        alpha, False)
    (y_p, k_p, v_p, lf_p, wkv_p, shift_p, conv_p) = prompt
    (y_s, k_s, v_s, lf_s, wkv_s, shift_s, conv_s) = sample
    return (y_p, y_s, k_p, v_p, lf_p, wkv_p, shift_p, conv_p,
            k_s, v_s, lf_s, wkv_s, shift_s, conv_s)
```

```python
import functools
import math

import jax
import jax.numpy as jnp
import numpy as np
from jax import lax
from jax.experimental import pallas as pl
from jax.experimental.pallas import tpu as pltpu

F32 = jnp.float32
BF16 = jnp.bfloat16

N_HEADS = 8
HEAD_DIM = 64
WIDTH = N_HEADS * HEAD_DIM
R_DECAY = 64
R_ICLR = 64
R_GATE = 128
C_RWKV = 3 * WIDTH + R_DECAY + R_ICLR + R_GATE
CONV_W = 3
PAGE_SIZE = 128
LN_EPS = 1e-5
GN_EPS = 64e-5
NEG_BIG = -1e30

LOG2E = math.log2(math.e)

LANES = 128
ROW_TILE = 256
ATTN_TILE = 512
ATTN_SUB = 256
V_ROWS = HEAD_DIM + 16
RWKV_CHUNK = 64
PAGES_PER_STEP = 16
VMEM_LIMIT_BYTES = 56 * 1024 * 1024

_OFF_Q, _OFF_K, _OFF_V, _OFF_F = 0, WIDTH, 2 * WIDTH, 3 * WIDTH
_OFF_RWKV = 3 * WIDTH + LANES
_OFF_GA = _OFF_RWKV + C_RWKV


def _iota(shape, dim):
    return lax.broadcasted_iota(jnp.int32, shape, dim)


def _dot(a, b):
    return jnp.dot(a.astype(BF16), b.astype(BF16), preferred_element_type=F32)


def _dot_nt(a, b):
    return lax.dot_general(a.astype(BF16), b.astype(BF16), (((1,), (1,)), ((), ())),
                           preferred_element_type=F32)


def _dot_tn(a, b):
    return lax.dot_general(a.astype(BF16), b.astype(BF16), (((0,), (0,)), ((), ())),
                           preferred_element_type=F32)


def _split3(x):
    hi = x.astype(BF16)
    r1 = x - hi.astype(F32)
    mid = r1.astype(BF16)
    lo = (r1 - mid.astype(F32)).astype(BF16)
    return hi, mid, lo


def _dot_sel_rhs(x, sel, pieces=3):
    hi, mid, lo = _split3(x)
    d = lambda p: jnp.dot(p, sel, preferred_element_type=F32)
    if pieces == 2:
        return d(hi) + d(mid)
    return d(hi) + (d(mid) + d(lo))


def _dot_sel_lhs(sel, x):
    hi, mid, lo = _split3(x)
    d = lambda p: jnp.dot(sel, p, preferred_element_type=F32)
    return d(hi) + (d(mid) + d(lo))


def _sigmoid(x):
    return 1.0 / (1.0 + jnp.exp(-x))


def _log_sigmoid(x):
    return jnp.minimum(x, 0.0) - jnp.log1p(jnp.exp(-jnp.abs(x)))


def _layer_norm(x, w, b):
    mu = jnp.mean(x, axis=-1, keepdims=True)
    d = x - mu
    var = jnp.mean(d * d, axis=-1, keepdims=True)
    return d * lax.rsqrt(var + LN_EPS) * w + b


def _params(*sem):
    return pltpu.CompilerParams(dimension_semantics=sem, vmem_limit_bytes=VMEM_LIMIT_BYTES)


def _const_spec(shape):
    zeros = (0,) * len(shape)
    return pl.BlockSpec(shape, lambda *_: zeros, pipeline_mode=pl.Buffered(1))


def _aug_constants():
    place_q = np.zeros((3 * LANES, N_HEADS * LANES), np.float32)
    place_k = np.zeros((3 * LANES, N_HEADS * LANES), np.float32)
    ones_q = np.zeros((1, N_HEADS * LANES), np.float32)
    ones_k = np.zeros((1, N_HEADS * LANES), np.float32)
    for h in range(N_HEADS):
        base = LANES * h + (HEAD_DIM if h % 2 == 0 else 0)
        for piece in range(3):
            place_q[piece * LANES + h, base + piece] = 1.0
            place_k[piece * LANES + h, base + 3 + piece] = -1.0
            ones_q[0, base + 3 + piece] = 1.0
            ones_k[0, base + piece] = 1.0
    return (jnp.asarray(place_q, BF16), jnp.asarray(place_k, BF16),
            jnp.asarray(ones_q), jnp.asarray(ones_k))


def _in_proj_kernel(x_ref, w_ref, bf_ref, *refs, tm, augment, seg):
    if augment:
        (pq_ref, pk_ref, oq_ref, ok_ref,
         q_ref, k_ref, v_ref, kaug_ref, vt_ref, logf_ref, c_ref, prw_ref, ga_ref, gb_ref, carry_ref) = refs
    else:
        (q_ref, k_ref, v_ref, logf_ref, c_ref, prw_ref, ga_ref, gb_ref, carry_ref) = refs
    t = pl.program_id(1)
    x = x_ref[0].astype(BF16)

    def proj(lo, hi):
        return jnp.dot(x, w_ref[:, lo:hi], preferred_element_type=F32)

    q = proj(_OFF_Q, _OFF_Q + WIDTH)
    k = proj(_OFF_K, _OFF_K + WIDTH)
    k_ref[0] = k
    v = proj(_OFF_V, _OFF_V + WIDTH)
    v_ref[0] = v
    if augment:
        vt = v.T.astype(BF16)
        extra = jnp.where(_iota((V_ROWS - HEAD_DIM, tm), 0) == 0, 1.0, 0.0).astype(BF16)
        vt_ref[0, 0] = jnp.concatenate(
            [z for h in range(N_HEADS) for z in (vt[HEAD_DIM * h:HEAD_DIM * (h + 1)], extra)], axis=0)
    else:
        q_ref[0] = q * HEAD_DIM ** -0.5

    z = proj(_OFF_F, _OFF_F + LANES) + bf_ref[...]
    lane = _iota((tm, LANES), 1)
    lf = jnp.where(lane < N_HEADS, _log_sigmoid(z), 0.0)
    logf_ref[0] = lf[:, :N_HEADS]

    @pl.when(t == 0)
    def _():
        carry_ref[...] = jnp.zeros_like(carry_ref)

    earlier = _iota((tm, tm), 1) <= _iota((tm, tm), 0)
    if seg is not None:
        earlier = earlier & (_iota((tm, tm), 1) // seg == _iota((tm, tm), 0) // seg)
    tril = jnp.where(earlier, 1.0, 0.0).astype(BF16)
    cs = _dot_sel_lhs(tril, lf) + carry_ref[...]
    carry_ref[...] = cs[tm - 1:tm, :]
    c_ref[0] = cs[:, :N_HEADS]

    if augment:
        pieces = jnp.concatenate(_split3(cs * LOG2E), axis=1)
        head_lane = _iota((tm, N_HEADS * LANES), 1)
        own = (head_lane % LANES) // HEAD_DIM == (head_lane // LANES) % 2
        pair = lambda z: jnp.concatenate(
            [z[:, LANES * (h // 2):LANES * (h // 2 + 1)] for h in range(N_HEADS)], axis=1)
        aug_q = jnp.dot(pieces, pq_ref[...], preferred_element_type=F32) + oq_ref[...]
        aug_k = jnp.dot(pieces, pk_ref[...], preferred_element_type=F32) + ok_ref[...]
        q_ref[0] = jnp.where(own, pair(q * (HEAD_DIM ** -0.5 * LOG2E)), aug_q).astype(BF16)
        kaug_ref[0] = jnp.where(own, pair(k), aug_k).astype(BF16)

    prw_ref[0] = proj(_OFF_RWKV, _OFF_RWKV + C_RWKV)
    d_model = ga_ref.shape[-1]
    ga_ref[0] = _sigmoid(proj(_OFF_GA, _OFF_GA + d_model))
    gb_ref[0] = _sigmoid(proj(_OFF_GA + d_model, _OFF_GA + 2 * d_model))


def _in_proj(x, w, bf, *, augment, seg=None):
    bsz, t, d = x.shape
    if not augment and bsz > 1 and t < ROW_TILE and bsz * t <= ROW_TILE:
        outs = _in_proj(x.reshape(1, bsz * t, d), w, bf, augment=False, seg=t)
        return [z.reshape(bsz, t, z.shape[-1]) for z in outs]
    tm = min(t, ROW_TILE)
    nt = t // tm
    c_all = w.shape[1]
    row = lambda width: pl.BlockSpec((1, tm, width), lambda b, i: (b, i, 0))
    in_specs = [row(d), _const_spec((d, c_all)), _const_spec((1, LANES))]
    operands = [x, w, bf]
    wide = N_HEADS * LANES
    if augment:
        per_tile = ATTN_TILE // tm
        consts = _aug_constants()
        in_specs += [_const_spec(c.shape) for c in consts]
        operands += list(consts)
        out_shape = [jax.ShapeDtypeStruct((bsz, t, wide), BF16),
                     jax.ShapeDtypeStruct((bsz, t, WIDTH), F32),
                     jax.ShapeDtypeStruct((bsz, t, WIDTH), F32),
                     jax.ShapeDtypeStruct((bsz, t, wide), BF16),
                     jax.ShapeDtypeStruct((bsz, t // ATTN_TILE, N_HEADS * V_ROWS, ATTN_TILE), BF16)]
        out_specs = [row(wide), row(WIDTH), row(WIDTH), row(wide),
                     pl.BlockSpec((1, 1, N_HEADS * V_ROWS, tm),
                                  lambda b, i: (b, i // per_tile, 0, i % per_tile))]
    else:
        out_shape = [jax.ShapeDtypeStruct((bsz, t, WIDTH), F32),
                     jax.ShapeDtypeStruct((bsz, t, WIDTH), F32),
                     jax.ShapeDtypeStruct((bsz, t, WIDTH), F32)]
        out_specs = [row(WIDTH), row(WIDTH), row(WIDTH)]
    out_shape += [jax.ShapeDtypeStruct((bsz, t, N_HEADS), F32),
                  jax.ShapeDtypeStruct((bsz, t, N_HEADS), F32),
                  jax.ShapeDtypeStruct((bsz, t, C_RWKV), F32),
                  jax.ShapeDtypeStruct((bsz, t, d), F32),
                  jax.ShapeDtypeStruct((bsz, t, d), F32)]
    out_specs += [row(N_HEADS), row(N_HEADS), row(C_RWKV), row(d), row(d)]
    return pl.pallas_call(
        functools.partial(_in_proj_kernel, tm=tm, augment=augment, seg=seg),
        grid=(bsz, nt),
        in_specs=in_specs,
        out_specs=out_specs,
        out_shape=out_shape,
        scratch_shapes=[pltpu.VMEM((1, LANES), F32)],
        compiler_params=_params("arbitrary", "arbitrary"),
        name="in_proj",
    )(*operands)


def _fox_prompt_kernel(q_ref, k_ref, vt_ref, o_ref, s_ref, m_ref, acc_ref, *, tile):
    i = pl.program_id(2)
    heads = (0, 1)
    key_pos = _iota((ATTN_SUB, tile), 0)
    query_pos = _iota((ATTN_SUB, tile), 1)

    pairs = [(sub, h) for sub in range(tile // ATTN_SUB) for h in heads]

    def score(j, slot):
        start = pl.multiple_of(j * tile, tile)
        for n, (sub, h) in enumerate(pairs):
            lanes = slice(LANES * h, LANES * (h + 1))
            s_ref[slot, n] = _dot_nt(k_ref[0, pl.ds(start + sub * ATTN_SUB, ATTN_SUB), lanes],
                                     q_ref[0, :, lanes])

    def consume(j, slot, masked):
        carry = [(m_ref[h], acc_ref[h]) for h in heads]
        probs = []
        scale = []
        for n, (sub, h) in enumerate(pairs):
            s = s_ref[slot, n]
            if masked:
                s = jnp.where(key_pos + sub * ATTN_SUB <= query_pos, s, NEG_BIG)
            m, acc = carry[h]
            m_new = jnp.maximum(m, jnp.max(s, axis=0, keepdims=True))
            probs.append(jnp.exp2(s - m_new).astype(BF16))
            scale.append(jnp.exp2(m - m_new))
            carry[h] = (m_new, acc)
        for n, (sub, h) in enumerate(pairs):
            m_new, acc = carry[h]
            vt = vt_ref[0, j, V_ROWS * h:V_ROWS * (h + 1), sub * ATTN_SUB:(sub + 1) * ATTN_SUB]
            carry[h] = (m_new, scale[n] * acc + jnp.dot(vt, probs[n], preferred_element_type=F32))
        for h in heads:
            m_ref[h], acc_ref[h] = carry[h]

    for h in heads:
        m_ref[h] = jnp.full((1, tile), NEG_BIG, F32)
        acc_ref[h] = jnp.zeros((V_ROWS, tile), F32)

    score(0, 0)

    def two_tiles(jj, _):
        j = 2 * jj
        score(j + 1, 1)
        consume(j, 0, False)
        score(j + 2, 0)
        consume(j + 1, 1, False)
        return 0

    lax.fori_loop(0, i // 2, two_tiles, 0)
    odd = i % 2 == 1

    @pl.when(odd)
    def _():
        score(i, 1)
        consume(i - 1, 0, False)
        consume(i, 1, True)

    @pl.when(jnp.logical_not(odd))
    def _():
        consume(i, 0, True)

    o_ref[0] = jnp.concatenate(
        [acc_ref[h, :HEAD_DIM] / acc_ref[h, HEAD_DIM:HEAD_DIM + 1] for h in heads], axis=0).T


def _fox_prompt(q_aug, k_aug, vt_bf):
    bsz, t, _ = q_aug.shape
    tile = vt_bf.shape[-1]
    nq = t // tile
    n_pairs = N_HEADS // 2
    return pl.pallas_call(
        functools.partial(_fox_prompt_kernel, tile=tile),
        grid=(bsz, n_pairs, nq),
        in_specs=[pl.BlockSpec((1, tile, 2 * LANES), lambda b, hp, i: (b, i, hp)),
                  pl.BlockSpec((1, t, 2 * LANES), lambda b, hp, i: (b, 0, hp)),
                  pl.BlockSpec((1, nq, 2 * V_ROWS, tile), lambda b, hp, i: (b, 0, hp, 0))],
        out_specs=pl.BlockSpec((1, tile, LANES), lambda b, hp, i: (b, i, hp)),
        out_shape=jax.ShapeDtypeStruct((bsz, t, WIDTH), F32),
        scratch_shapes=[pltpu.VMEM((2, 2 * (tile // ATTN_SUB), ATTN_SUB, tile), F32),
                        pltpu.VMEM((2, 1, tile), F32), pltpu.VMEM((2, V_ROWS, tile), F32)],
        compiler_params=_params("arbitrary", "arbitrary", "arbitrary"),
        name="fox_prompt",
    )(q_aug, k_aug, vt_bf)


def _fox_cached_kernel(pt_ref, q_ref, knew_ref, vnew_ref, cq_ref, ck_ref, later_ref, *refs, n_group):
    del pt_ref
    kp = refs[:n_group]
    vp = refs[n_group:2 * n_group]
    lp = refs[2 * n_group:3 * n_group]
    o_ref, m_ref, l_ref, acc_ref, carry_ref, both_ref = refs[3 * n_group:]
    step = pl.program_id(1)
    t_new = q_ref.shape[1]
    rows = N_HEADS * t_new
    own_head = _iota((rows, WIDTH), 0) // t_new == _iota((rows, WIDTH), 1) // HEAD_DIM
    qx = jnp.where(own_head, jnp.concatenate([q_ref[0]] * N_HEADS, axis=0), 0.0).astype(BF16)

    @pl.when(step == 0)
    def _():
        s = _dot_nt(qx, knew_ref[0])
        s = s + (cq_ref[0] - ck_ref[0])
        qpos = _iota((rows, t_new), 0) % t_new
        s = jnp.where(_iota((rows, t_new), 1) <= qpos, s, NEG_BIG)
        m = jnp.max(s, axis=1, keepdims=True)
        p = jnp.exp(s - m)
        m_ref[...] = m
        l_ref[...] = jnp.sum(p, axis=1, keepdims=True)
        acc_ref[...] = _dot(p, vnew_ref[0])
        carry_ref[...] = jnp.broadcast_to(cq_ref[0], carry_ref.shape)

    groups = range(n_group)
    raw = [_dot(qx, kp[g][0]) for g in groups]
    lf_t = jnp.concatenate([lp[g][0] for g in groups], axis=0)
    both_ref[...] = _dot_sel_rhs(lf_t, later_ref[...])
    carry = carry_ref[...]
    scores = []
    for g in groups:
        both = jnp.concatenate(
            [jnp.broadcast_to(both_ref[g * N_HEADS + h:g * N_HEADS + h + 1, :], (t_new, 2 * PAGE_SIZE))
             for h in range(N_HEADS)], axis=0)
        scores.append(raw[g] + (both[:, :PAGE_SIZE] + carry))
        carry = carry + both[:, PAGE_SIZE:]
    carry_ref[...] = carry

    smax = scores[0]
    for s in scores[1:]:
        smax = jnp.maximum(smax, s)
    m_old = m_ref[...]
    m_new = jnp.maximum(m_old, jnp.max(smax, axis=1, keepdims=True))
    alpha = jnp.exp(m_old - m_new)
    l = alpha * l_ref[...]
    acc = alpha * acc_ref[...]
    probs = [jnp.exp(scores[g] - m_new) for g in groups]
    for g in groups:
        l = l + jnp.sum(probs[g], axis=1, keepdims=True)
        acc = acc + _dot_nt(probs[g], vp[g][0])
    m_ref[...] = m_new
    l_ref[...] = l
    acc_ref[...] = acc

    @pl.when(step == pl.num_programs(1) - 1)
    def _():
        o = jnp.where(own_head, acc / l, 0.0)
        out = o[0:t_new]
        for h in range(1, N_HEADS):
            out = out + o[h * t_new:(h + 1) * t_new]
        o_ref[0] = out


def _fox_cached(q_bf, k_new, v_new, c_new, cache_k, cache_v, cache_logf, pages):
    bsz, t_new, _ = q_bf.shape
    n_pages = pages.shape[1]
    n_group = math.gcd(PAGES_PER_STEP, n_pages)
    n_steps = n_pages // n_group
    rows = N_HEADS * t_new
    c_q = c_new.transpose(0, 2, 1).reshape(bsz, rows, 1)
    c_k = jnp.repeat(c_new.transpose(0, 2, 1), t_new, axis=1)

    def page_map(g, n_trailing):
        zeros = (0,) * n_trailing
        return lambda b, s, pt: (pt[b, n_pages - 1 - (s * n_group + g)],) + zeros

    def per_b(shape):
        zeros = (0,) * len(shape)
        return pl.BlockSpec((1,) + shape, lambda b, s, pt: (b,) + zeros)

    keys = np.arange(PAGE_SIZE)
    later = np.concatenate([(keys[:, None] > keys[None, :]).astype(np.float32),
                            np.ones((PAGE_SIZE, PAGE_SIZE), np.float32)], axis=1)
    later = jnp.asarray(later, BF16)

    tok_blk = (t_new, WIDTH)
    in_specs = [per_b(tok_blk), per_b(tok_blk), per_b(tok_blk), per_b((rows, 1)), per_b((rows, t_new)),
                pl.BlockSpec(later.shape, lambda b, s, pt: (0, 0), pipeline_mode=pl.Buffered(1))]
    kv_blk = (1, WIDTH, PAGE_SIZE)
    in_specs += [pl.BlockSpec(kv_blk, page_map(g, 2)) for g in range(n_group)]
    in_specs += [pl.BlockSpec(kv_blk, page_map(g, 2)) for g in range(n_group)]
    in_specs += [pl.BlockSpec((1, N_HEADS, PAGE_SIZE), page_map(g, 2)) for g in range(n_group)]
    return pl.pallas_call(
        functools.partial(_fox_cached_kernel, n_group=n_group),
        grid_spec=pltpu.PrefetchScalarGridSpec(
            num_scalar_prefetch=1,
            grid=(bsz, n_steps),
            in_specs=in_specs,
            out_specs=per_b(tok_blk),
            scratch_shapes=[pltpu.VMEM((rows, 1), F32), pltpu.VMEM((rows, 1), F32),
                            pltpu.VMEM((rows, WIDTH), F32), pltpu.VMEM((rows, PAGE_SIZE), F32),
                            pltpu.VMEM((n_group * N_HEADS, later.shape[1]), F32)]),
        out_shape=jax.ShapeDtypeStruct((bsz,) + tok_blk, F32),
        compiler_params=_params("arbitrary", "arbitrary"),
        name="fox_cached",
    )(pages, q_bf, k_new, v_new, c_q, c_k, later,
      *([cache_k] * n_group), *([cache_v] * n_group), *([cache_logf] * n_group))


def _unit_lower_inverse(n_mats, eye, blk_of_row, blk_of_col, chunk):
    base = 8
    same = lambda size: (blk_of_row // (size // base)) == (blk_of_col // (size // base))
    ms = [jnp.where(same(base), n, 0.0) for n in n_mats]
    xs = [eye + m for m in ms]
    for _ in range(2):
        ms = [_dot(m, m) for m in ms]
        xs = [x + _dot(x, m) for x, m in zip(xs, ms)]
    size = base
    while size < chunk:
        ring = same(2 * size) & jnp.logical_not(same(size))
        half = [_dot(x, jnp.where(ring, n, 0.0)) for x, n in zip(xs, n_mats)]
        xs = [x + _dot(h, x) for x, h in zip(xs, half)]
        size *= 2
    return xs


def _rwkv_kernel(p_ref, shift0_ref, wkv0_ref, mu_ref, w0_ref, lora_ref, a0_ref, gup_ref,
                 kk_ref, ka_ref, rk_ref, gnw_ref, gnb_ref,
                 o_ref, wkv_ref, shift_ref,
                 state_ref, prev_ref, pre_ref, y_ref, *, chunk, heads_per_group, tb):
    t = pl.program_id(1)
    gw = heads_per_group * HEAD_DIM
    n_groups = N_HEADS // heads_per_group
    gl = heads_per_group * chunk
    n_chunks = tb // chunk

    row_head = _iota((gw, gw), 0) // HEAD_DIM
    col_head = _iota((gw, gw), 1) // HEAD_DIM
    state_mask = row_head == col_head

    @pl.when(t == 0)
    def _():
        prev_ref[...] = shift0_ref[0]
        tile_lanes = jnp.where(_iota((HEAD_DIM, gw), 0) == _iota((HEAD_DIM, gw), 1) % HEAD_DIM,
                               1.0, 0.0).astype(BF16)
        for g in range(n_groups):
            s0 = wkv0_ref[0, g * gw:(g + 1) * gw, :]
            state_ref[g] = jnp.where(state_mask, _dot_sel_rhs(s0, tile_lanes), 0.0)

    p = p_ref[0]
    first_row = _iota((tb, C_RWKV), 0) == 0
    shifted = jnp.where(first_row, prev_ref[...], pltpu.roll(p, 1, 0))
    prev_ref[...] = p[tb - 1:tb, :]
    shift_ref[0] = p[tb - 1:tb, :]
    seg_r = _iota((WIDTH, WIDTH), 0) // HEAD_DIM
    seg_c = _iota((WIDTH, WIDTH), 1) // HEAD_DIM
    head_ones = jnp.where(seg_r == seg_c, 1.0, 0.0).astype(BF16)

    def pre():
        rows = slice(0, tb)
        xs = p + mu_ref[...] * (shifted - p)
        r = xs[:, 0:WIDTH]
        k = xs[:, WIDTH:2 * WIDTH]
        v = xs[:, 2 * WIDTH:3 * WIDTH]
        wa = xs[:, 3 * WIDTH:3 * WIDTH + LANES]
        gd = xs[:, 3 * WIDTH + LANES:]
        lane = _iota((tb, LANES), 1)
        lora = _dot(jnp.where(lane < R_DECAY, jnp.tanh(wa), wa), lora_ref[...])
        w_raw = _log_sigmoid(w0_ref[...] + lora[:, :WIDTH]) - 0.5
        lw = -jnp.exp(w_raw)
        iclr = _sigmoid(a0_ref[...] + lora[:, WIDTH:])
        kk = k * kk_ref[...]
        kk_norm = jnp.sqrt(_dot_sel_rhs(kk * kk, head_ones, pieces=2))
        kk = kk / jnp.maximum(kk_norm, 1e-12)
        k2 = k * (1.0 + (iclr - 1.0) * ka_ref[...])
        a = -kk
        b = kk * iclr
        sq = (tb, tb)
        incl = jnp.where((_iota(sq, 0) // chunk == _iota(sq, 1) // chunk) & (_iota(sq, 1) <= _iota(sq, 0)),
                         1.0, 0.0).astype(BF16)
        cum = _dot_sel_lhs(incl, lw)
        cum_end = jnp.concatenate(
            [jnp.broadcast_to(cum[(c + 1) * chunk - 1:(c + 1) * chunk, :], (chunk, WIDTH))
             for c in range(n_chunks)], axis=0)
        pre_ref[0, rows] = a * jnp.exp(cum - lw)
        pre_ref[1, rows] = r * jnp.exp(cum)
        pre_ref[2, rows] = b * jnp.exp(-cum)
        pre_ref[3, rows] = k2 * jnp.exp(-cum)
        pre_ref[4, rows] = b * jnp.exp(cum_end - cum)
        pre_ref[5, rows] = k2 * jnp.exp(cum_end - cum)
        pre_ref[6, rows] = v
        pre_ref[7, rows] = jnp.exp(cum_end)
        pre_ref[8, rows] = r * k2 * rk_ref[...]
        pre_ref[9, rows] = _dot(_sigmoid(gd), gup_ref[...])

    e_row_head = _iota((gl, gw), 0) // chunk
    e_col_head = _iota((gl, gw), 1) // HEAD_DIM
    expand_mask = e_row_head == e_col_head
    sq_r = _iota((gl, gl), 0)
    sq_c = _iota((gl, gl), 1)
    same_head = (sq_r // chunk) == (sq_c // chunk)
    strict = same_head & (sq_c < sq_r)
    lower = same_head & (sq_c <= sq_r)
    eye = jnp.where(sq_r == sq_c, 1.0, 0.0)
    blk8_r = sq_r // 8
    blk8_c = sq_c // 8

    def expand(x):
        return jnp.where(expand_mask, jnp.concatenate([x] * heads_per_group, axis=0), 0.0)

    def collapse(x):
        out = x[0:chunk]
        for h in range(1, heads_per_group):
            out = out + x[h * chunk:(h + 1) * chunk]
        return out

    def tile_rows(x):
        return jnp.concatenate([x] * heads_per_group, axis=0)

    def ld(idx, c, g, n_rows=chunk):
        return pre_ref[idx, c * chunk:c * chunk + n_rows, g * gw:(g + 1) * gw]

    def independent():
        pairs = [(c, g) for c in range(n_chunks) for g in range(n_groups)]
        ax = {cg: expand(ld(0, *cg)).astype(BF16) for cg in pairs}
        rx = {cg: expand(ld(1, *cg)).astype(BF16) for cg in pairs}
        bt = {cg: tile_rows(ld(2, *cg)).astype(BF16) for cg in pairs}
        kt = {cg: tile_rows(ld(3, *cg)).astype(BF16) for cg in pairs}
        vx = {cg: expand(ld(6, *cg)).astype(BF16) for cg in pairs}
        n_mats = [jnp.where(strict, _dot_nt(ax[cg], bt[cg]), 0.0) for cg in pairs]
        a_ak = {cg: jnp.where(strict, _dot_nt(ax[cg], kt[cg]), 0.0).astype(BF16) for cg in pairs}
        a_rb = {cg: jnp.where(lower, _dot_nt(rx[cg], bt[cg]), 0.0).astype(BF16) for cg in pairs}
        a_rk = {cg: jnp.where(lower, _dot_nt(rx[cg], kt[cg]), 0.0).astype(BF16) for cg in pairs}
        t_inv = dict(zip(pairs, (x.astype(BF16)
                                 for x in _unit_lower_inverse(n_mats, eye, blk8_r, blk8_c, chunk))))
        akv = {cg: _dot(a_ak[cg], vx[cg]) for cg in pairs}
        rkv = {cg: _dot(a_rk[cg], vx[cg]) for cg in pairs}
        return t_inv, a_rb, akv, rkv

    def chain(mats, state):
        t_inv, a_rb, akv, rkv = mats
        for c in range(n_chunks):
            ux = {}
            for g in range(n_groups):
                rhs = expand(_dot_nt(ld(0, c, g), state[g])) + akv[c, g]
                ux[g] = _dot(t_inv[c, g], rhs)
            for g in range(n_groups):
                y = _dot_nt(ld(1, c, g), state[g]) + collapse(_dot(a_rb[c, g], ux[g]) + rkv[c, g])
                y_ref[c * chunk:(c + 1) * chunk, g * gw:(g + 1) * gw] = y
            for g in range(n_groups):
                uv = jnp.concatenate([collapse(ux[g]), ld(6, c, g)], axis=0)
                bk = jnp.concatenate([ld(4, c, g), ld(5, c, g)], axis=0)
                state[g] = state[g] * ld(7, c, g, 1) + jnp.where(state_mask, _dot_tn(uv, bk), 0.0)
        return state

    def post():
        rows = slice(0, tb)
        y = y_ref[rows]
        head_avg = (head_ones.astype(F32) * (1.0 / HEAD_DIM)).astype(BF16)
        mu_y = _dot_sel_rhs(y, head_avg)
        d = y - mu_y
        var_y = _dot_sel_rhs(d * d, head_avg, pieces=2)
        yn = d * lax.rsqrt(var_y + GN_EPS) * gnw_ref[...] + gnb_ref[...]
        bonus = _dot_sel_rhs(pre_ref[8, rows], head_ones, pieces=2)
        o_ref[0, rows] = (yn + bonus * pre_ref[6, rows]) * pre_ref[9, rows]

    pre()
    state = chain(independent(), [state_ref[g] for g in range(n_groups)])
    post()
    for g in range(n_groups):
        state_ref[g] = state[g]

    @pl.when(t == pl.num_programs(1) - 1)
    def _():
        pick = jnp.where(_iota((gw, HEAD_DIM), 0) % HEAD_DIM == _iota((gw, HEAD_DIM), 1),
                         1.0, 0.0).astype(BF16)
        for g in range(n_groups):
            wkv_ref[0, g * gw:(g + 1) * gw, :] = _dot_sel_rhs(state_ref[g], pick)


def _rwkv(p_rwkv, shift0, wkv0, prm):
    bsz, t, _ = p_rwkv.shape
    if t >= RWKV_CHUNK:
        chunk, heads_per_group, tb = RWKV_CHUNK, 4, min(t, 2 * ROW_TILE)
    else:
        chunk, heads_per_group, tb = t, N_HEADS, t
    gw = heads_per_group * HEAD_DIM
    n_groups = N_HEADS // heads_per_group
    row = lambda width: pl.BlockSpec((1, tb, width), lambda b, i: (b, i, 0))
    per_b = lambda shape: pl.BlockSpec((1,) + shape, lambda b, i: (b, 0, 0))
    vec = lambda width: _const_spec((1, width))
    o, wkv, shift = pl.pallas_call(
        functools.partial(_rwkv_kernel, chunk=chunk, heads_per_group=heads_per_group, tb=tb),
        grid=(bsz, t // tb),
        in_specs=[row(C_RWKV), per_b((1, C_RWKV)), per_b((WIDTH, HEAD_DIM)),
                  vec(C_RWKV), vec(WIDTH), _const_spec((LANES, 2 * WIDTH)), vec(WIDTH),
                  _const_spec((R_GATE, WIDTH)), vec(WIDTH), vec(WIDTH), vec(WIDTH), vec(WIDTH),
                  vec(WIDTH)],
        out_specs=[row(WIDTH), per_b((WIDTH, HEAD_DIM)), per_b((1, C_RWKV))],
        out_shape=[jax.ShapeDtypeStruct((bsz, t, WIDTH), F32),
                   jax.ShapeDtypeStruct((bsz, WIDTH, HEAD_DIM), F32),
                   jax.ShapeDtypeStruct((bsz, 1, C_RWKV), F32)],
        scratch_shapes=[pltpu.VMEM((n_groups, gw, gw), F32), pltpu.VMEM((1, C_RWKV), F32),
                        pltpu.VMEM((10, tb, WIDTH), F32), pltpu.VMEM((tb, WIDTH), F32)],
        compiler_params=_params("arbitrary", "arbitrary"),
        name="rwkv7",
    )(p_rwkv, shift0.reshape(bsz, 1, C_RWKV), wkv0.reshape(bsz, WIDTH, HEAD_DIM),
      prm["mu"], prm["w0"], prm["lora"], prm["a0"], prm["gup"], prm["kk"], prm["ka"], prm["rk"],
      prm["gnw"], prm["gnb"])
    return (o, wkv.reshape(bsz, N_HEADS, HEAD_DIM, HEAD_DIM), shift.reshape(bsz, C_RWKV))


def _merge_kernel(x_ref, oa_ref, ob_ref, ga_ref, gb_ref, wa_ref, wb_ref, wo_ref, lnw_ref, lnb_ref,
                  y_ref, *, alpha):
    merged = ga_ref[0] * _dot(oa_ref[0], wa_ref[...]) + gb_ref[0] * _dot(ob_ref[0], wb_ref[...])
    y = alpha * x_ref[0] + _dot(merged, wo_ref[...])
    y_ref[0] = _layer_norm(y, lnw_ref[...], lnb_ref[...])


def _merge(x, o_a, o_b, g_a, g_b, w_a, w_b, w_o, ln_w, ln_b, alpha):
    bsz, t, d = x.shape
    if t < ROW_TILE and bsz > 1 and bsz * t <= ROW_TILE:
        flat = lambda z: z.reshape(1, bsz * t, z.shape[-1])
        y = _merge(flat(x), flat(o_a), flat(o_b), flat(g_a), flat(g_b), w_a, w_b, w_o, ln_w, ln_b, alpha)
        return y.reshape(bsz, t, d)
    tm = min(t, ROW_TILE)
    row = lambda width: pl.BlockSpec((1, tm, width), lambda b, i: (b, i, 0))
    return pl.pallas_call(
        functools.partial(_merge_kernel, alpha=alpha),
        grid=(bsz, t // tm),
        in_specs=[row(d), row(WIDTH), row(WIDTH), row(d), row(d),
                  _const_spec(w_a.shape), _const_spec(w_b.shape), _const_spec(w_o.shape),
                  _const_spec((1, d)), _const_spec((1, d))],
        out_specs=row(d),
        out_shape=jax.ShapeDtypeStruct((bsz, t, d), F32),
        compiler_params=_params("arbitrary", "arbitrary"),
        name="merge_ln",
    )(x, o_a, o_b, g_a, g_b, w_a, w_b, w_o, ln_w, ln_b)


def _gelu_tanh(x):
    return x * (0.5 * (1.0 + jnp.tanh(math.sqrt(2.0 / math.pi) * (x + 0.044715 * (x * x * x)))))


def _ffn_kernel(x_ref, *refs, alpha, tm, seg):
    if seg is None:
        (conv0_ref, wup_ref, cw_ref, cb_ref, wdn_ref, lnw_ref, lnb_ref, y_ref, conv_ref, tail_ref) = refs
    else:
        (back1_ref, back2_ref, wup_ref, cw_ref, cb_ref, wdn_ref, lnw_ref, lnb_ref, y_ref, conv_ref) = refs
    d_ff = cb_ref.shape[-1]
    n_tail = CONV_W - 1

    if seg is None:
        @pl.when(pl.program_id(1) == 0)
        def _():
            tail_ref[...] = conv0_ref[0]

    x = x_ref[0]
    xb = x.astype(BF16)
    u = jnp.dot(xb, wup_ref[:, :d_ff], preferred_element_type=F32)
    gate = jnp.dot(xb, wup_ref[:, d_ff:], preferred_element_type=F32)
    rowi = _iota((tm, d_ff), 0)
    conv = cb_ref[...] + cw_ref[n_tail:n_tail + 1, :] * u
    if seg is None:
        for back in range(1, CONV_W):
            prev = pltpu.roll(u, back, 0)
            for j in range(back):
                prev = jnp.where(rowi == j, tail_ref[n_tail - back + j:n_tail - back + j + 1, :], prev)
            conv = conv + cw_ref[n_tail - back:n_tail - back + 1, :] * prev
        tail_ref[...] = u[tm - n_tail:tm, :]
        conv_ref[0] = u[tm - n_tail:tm, :]
    else:
        pos = rowi % seg
        for back, back_ref in ((1, back1_ref), (2, back2_ref)):
            prev = jnp.where(pos >= back, pltpu.roll(u, back, 0), back_ref[0])
            conv = conv + cw_ref[n_tail - back:n_tail - back + 1, :] * prev
        conv_ref[0] = u
    hmid = _gelu_tanh(conv) * gate
    y = alpha * x + _dot(hmid, wdn_ref[...])
    y_ref[0] = _layer_norm(y, lnw_ref[...], lnb_ref[...])


def _ffn_short(x, conv0, w_up, conv_w, conv_b, w_down, ln_w, ln_b, alpha):
    bsz, t, d = x.shape
    d_ff = w_down.shape[0]
    rows = bsz * t
    assert CONV_W == 3 and t >= CONV_W - 1 and rows <= ROW_TILE
    zeros = jnp.zeros((bsz, t, d_ff), F32)
    back1 = zeros.at[:, 0].set(conv0[:, 1]).reshape(1, rows, d_ff)
    back2 = zeros.at[:, 0].set(conv0[:, 0]).at[:, 1].set(conv0[:, 1]).reshape(1, rows, d_ff)
    whole = lambda width: pl.BlockSpec((1, rows, width), lambda i: (0, 0, 0))
    y, u = pl.pallas_call(
        functools.partial(_ffn_kernel, alpha=alpha, tm=rows, seg=t),
        grid=(1,),
        in_specs=[whole(d), whole(d_ff), whole(d_ff), _const_spec(w_up.shape),
                  _const_spec((CONV_W, d_ff)), _const_spec((1, d_ff)), _const_spec(w_down.shape),
                  _const_spec((1, d)), _const_spec((1, d))],
        out_specs=[whole(d), whole(d_ff)],
        out_shape=[jax.ShapeDtypeStruct((1, rows, d), F32), jax.ShapeDtypeStruct((1, rows, d_ff), F32)],
        compiler_params=_params("arbitrary"),
        name="conv_ffn_ln",
    )(x.reshape(1, rows, d), back1, back2, w_up, conv_w, conv_b, w_down, ln_w, ln_b)
    return y.reshape(bsz, t, d), u.reshape(bsz, t, d_ff)[:, t - (CONV_W - 1):]


def _ffn(x, conv0, w_up, conv_w, conv_b, w_down, ln_w, ln_b, alpha):
    bsz, t, d = x.shape
    d_ff = w_down.shape[0]
    if t < ROW_TILE and bsz * t <= ROW_TILE:
        return _ffn_short(x, conv0, w_up, conv_w, conv_b, w_down, ln_w, ln_b, alpha)
    tm = min(t, ROW_TILE)
    assert tm >= CONV_W - 1
    row = lambda width: pl.BlockSpec((1, tm, width), lambda b, i: (b, i, 0))
    per_b = lambda shape: pl.BlockSpec((1,) + shape, lambda b, i: (b, 0, 0))
    return pl.pallas_call(
        functools.partial(_ffn_kernel, alpha=alpha, tm=tm, seg=None),
        grid=(bsz, t // tm),
        in_specs=[row(d), per_b((CONV_W - 1, d_ff)), _const_spec(w_up.shape),
                  _const_spec((CONV_W, d_ff)), _const_spec((1, d_ff)), _const_spec(w_down.shape),
                  _const_spec((1, d)), _const_spec((1, d))],
        out_specs=[row(d), per_b((CONV_W - 1, d_ff))],
        out_shape=[jax.ShapeDtypeStruct((bsz, t, d), F32),
                   jax.ShapeDtypeStruct((bsz, CONV_W - 1, d_ff), F32)],
        scratch_shapes=[pltpu.VMEM((CONV_W - 1, d_ff), F32)],
        compiler_params=_params("arbitrary", "arbitrary"),
        name="conv_ffn_ln",
    )(x, conv0, w_up, conv_w, conv_b, w_down, ln_w, ln_b)


def _layer_params(l, w_in, b_forget, mu_shift, decay_w0, decay_up, iclr_a0, iclr_up, gate_up,
                  k_k, k_a, r_k, gn_w, gn_b, w_branch_a, w_branch_b, w_out, ln1_w, ln1_b,
                  w_up, conv_w, conv_b, w_down, ln2_w, ln2_b):
    d = w_in.shape[1]
    c_fox = 3 * WIDTH + N_HEADS
    w = w_in[l]
    w_pad = jnp.concatenate([w[:, :c_fox], jnp.zeros((d, LANES - N_HEADS), F32), w[:, c_fox:]], axis=1)
    bf = jnp.zeros((1, LANES), F32).at[0, :N_HEADS].set(b_forget[l])
    zeros = jnp.zeros((R_DECAY, WIDTH), F32)
    lora = jnp.concatenate([jnp.concatenate([decay_up[l], zeros], axis=1),
                            jnp.concatenate([zeros, iclr_up[l]], axis=1)], axis=0)
    vec = lambda z: z.reshape(1, -1)
    return dict(
        w_in=w_pad.astype(BF16), bf=bf,
        rwkv=dict(mu=vec(mu_shift[l]), w0=vec(decay_w0[l]), lora=lora.astype(BF16), a0=vec(iclr_a0[l]),
                  gup=gate_up[l].astype(BF16), kk=vec(k_k[l]), ka=vec(k_a[l]), rk=vec(r_k[l]),
                  gnw=vec(gn_w[l]), gnb=vec(gn_b[l])),
        w_a=w_branch_a[l].astype(BF16), w_b=w_branch_b[l].astype(BF16), w_o=w_out[l].astype(BF16),
        ln1_w=vec(ln1_w[l]), ln1_b=vec(ln1_b[l]),
        w_up=w_up[l].astype(BF16), conv_w=conv_w[l], conv_b=vec(conv_b[l]), w_down=w_down[l].astype(BF16),
        ln2_w=vec(ln2_w[l]), ln2_b=vec(ln2_b[l]))


def _run_group(x, layers, attend, shift0, wkv0, conv0, alpha, augment):
    bsz, t, _ = x.shape
    ks, vs, lfs, wkvs, shifts, convs = [], [], [], [], [], []
    for l, prm in enumerate(layers):
        outs = _in_proj(x, prm["w_in"], prm["bf"], augment=augment)
        if augment:
            q_aug, k, v, k_aug, vt_bf, logf, c, p_rwkv, g_a, g_b = outs
            o_a = attend(l, q_aug, k_aug, vt_bf, c)
        else:
            q_bf, k, v, logf, c, p_rwkv, g_a, g_b = outs
            o_a = attend(l, q_bf, k, v, c)
        o_b, wkv_l, shift_l = _rwkv(p_rwkv, shift0[l], wkv0[l], prm["rwkv"])
        x = _merge(x, o_a, o_b, g_a, g_b, prm["w_a"], prm["w_b"], prm["w_o"],
                   prm["ln1_w"], prm["ln1_b"], alpha)
        x, conv_l = _ffn(x, conv0[l], prm["w_up"], prm["conv_w"], prm["conv_b"], prm["w_down"],
                         prm["ln2_w"], prm["ln2_b"], alpha)
        ks.append(k.reshape(bsz, t, N_HEADS, HEAD_DIM))
        vs.append(v.reshape(bsz, t, N_HEADS, HEAD_DIM))
        lfs.append(logf)
        wkvs.append(wkv_l)
        shifts.append(shift_l)
        convs.append(conv_l)
    return (x, jnp.stack(ks), jnp.stack(vs), jnp.stack(lfs), jnp.stack(wkvs),
            jnp.stack(shifts), jnp.stack(convs))


def kernel(x_prompt, x_sample, cache_k, cache_v, cache_logf, state_wkv, state_shift, state_conv, page_table, w_in, b_forget, mu_shift, decay_w0, decay_up, iclr_a0, iclr_up, gate_up, k_k, k_a, r_k, gn_w, gn_b, w_branch_a, w_branch_b, w_out, ln1_w, ln1_b, w_up, conv_w, conv_b, w_down, ln2_w, ln2_b):
    depth = w_in.shape[0]
    alpha = (2.0 * depth) ** 0.25
    layers = [_layer_params(l, w_in, b_forget, mu_shift, decay_w0, decay_up, iclr_a0, iclr_up,
                            gate_up, k_k, k_a, r_k, gn_w, gn_b, w_branch_a, w_branch_b, w_out,
                            ln1_w, ln1_b, w_up, conv_w, conv_b, w_down, ln2_w, ln2_b)
              for l in range(depth)]

    n_pool = cache_k.shape[1]
    flat_k = cache_k.transpose(0, 1, 3, 4, 2).reshape(depth * n_pool, WIDTH, PAGE_SIZE)
    flat_v = cache_v.transpose(0, 1, 3, 4, 2).reshape(depth * n_pool, WIDTH, PAGE_SIZE)
    flat_lf = cache_logf.transpose(0, 1, 3, 2).reshape(depth * n_pool, N_HEADS, PAGE_SIZE)

    def attend_prompt(l, q_aug, k_aug, vt_bf, c):
        return _fox_prompt(q_aug, k_aug, vt_bf)

    def attend_sample(l, q_bf, k, v, c):
        return _fox_cached(q_bf, k, v, c, flat_k, flat_v, flat_lf, page_table + l * n_pool)

    bp, _, d = x_prompt.shape
    d_ff = w_down.shape[1]
    prompt = _run_group(x_prompt, layers, attend_prompt,
                        jnp.zeros((depth, bp, C_RWKV), F32),
                        jnp.zeros((depth, bp, N_HEADS, HEAD_DIM, HEAD_DIM), F32),
                        jnp.zeros((depth, bp, CONV_W - 1, d_ff), F32), alpha, True)
    sample = _run_group(x_sample, layers, attend_sample, state_shift, state_wkv, state_conv,
                        alpha, False)
    (y_p, k_p, v_p, lf_p, wkv_p, shift_p, conv_p) = prompt
    (y_s, k_s, v_s, lf_s, wkv_s, shift_s, conv_s) = sample
    return (y_p, y_s, k_p, v_p, lf_p, wkv_p, shift_p, conv_p,
            k_s, v_s, lf_s, wkv_s, shift_s, conv_s)
```
